```python
import math
import jax, jax.numpy as jnp
from jax import lax
import numpy as np

D_MODEL = 2048
BATCH = 1
SEQ = 8192
DEPTH = 2
DEC_BATCH = 32
DEC_SEQ = 1
PAST_LEN = 8192
PAGE_SIZE = 128

HEAD_DIM = 128
A_GROUPS = ((128, 1), (512, 4), (2048, 16))
N_GROUPS_A = len(A_GROUPS)
H_A = 8
A_Q_BLOCK = 128
H_B = 16
MOBA_BLOCK = 256
MOBA_TOPK = 3
MOBA_Q_CHUNK = 16
N_BUCKETS = 32
REL_MAX_DIST = 4096
D_FF = 5632
FFN_RES = 0.5
N_A_LAYERS = (DEPTH + 1) // 2
N_B_LAYERS = DEPTH // 2
RMS_EPS = 1e-6
NEG_INF = -1e30
ATTN_SCALE = HEAD_DIM ** -0.5

kernel_name = 'hybrid_dilated_moba_macaron_adaln_step'


def rms_norm(x, g):
    xf = x.astype(jnp.float32)
    y = xf * lax.rsqrt(jnp.mean(xf * xf, axis=-1, keepdims=True) + RMS_EPS)
    return (y * g.astype(jnp.float32)).astype(x.dtype)


def t5_bucket(dist):
    n = jnp.maximum(dist, 0)
    max_exact = N_BUCKETS // 2
    nf = jnp.maximum(n, 1).astype(jnp.float32)
    large = max_exact + (jnp.log(nf / max_exact) / math.log(REL_MAX_DIST / max_exact)
                         * (N_BUCKETS - max_exact)).astype(jnp.int32)
    large = jnp.minimum(large, N_BUCKETS - 1)
    return jnp.where(n < max_exact, n, large)


def masked_softmax_lse(logits, mask):
    logits = jnp.where(mask, logits, NEG_INF)
    m = jnp.max(logits, axis=-1, keepdims=True)
    e = jnp.exp(logits - m)
    s = jnp.sum(e, axis=-1, keepdims=True)
    return e / s, (m + jnp.log(s))[..., 0]


def swiglu(x, w_in, w_out):
    g, u = jnp.split(x @ w_in, 2, axis=-1)
    return (jax.nn.silu(g) * u) @ w_out


def ada_mod(c, w, b):
    return (jax.nn.silu(c) @ w + b).reshape(c.shape[0], 3, 3, D_MODEL)


def norm_modulate(x, g, mod, j):
    return rms_norm(x, g) * (1 + mod[:, j, 1][:, None, :]) + mod[:, j, 0][:, None, :]


def ffn_sublayer(x, g, mod, j, w_in, w_out):
    h = norm_modulate(x, g, mod, j)
    return x + FFN_RES * mod[:, j, 2][:, None, :] * swiglu(h, w_in, w_out)


def dilated_prompt(q, k, v, win, dil, bias_g):
    b, s, h, dh = q.shape
    n_back = win // dil
    qbl = A_Q_BLOCK
    span = dil * qbl
    sp = -(-s // span) * span
    m = sp // dil
    nbk = m // qbl

    def to_blocks(t):
        t = jnp.pad(t, ((0, 0), (0, sp - s), (0, 0), (0, 0)))
        t = t.reshape(b, m, dil, h, dh).transpose(0, 2, 1, 3, 4)
        return t.reshape(b, dil, nbk, qbl, h, dh)

    def with_prev(t):
        prev = jnp.pad(t, ((0, 0), (0, 0), (1, 0), (0, 0), (0, 0), (0, 0)))[:, :, :-1]
        return jnp.concatenate([prev, t], axis=3)

    qb = to_blocks(q)
    kk = with_prev(to_blocks(k))
    vv = with_prev(to_blocks(v))
    qi = qbl + jnp.arange(qbl)
    kj = jnp.arange(2 * qbl)
    diff = qi[:, None] - kj[None, :]
    band = (diff >= 0) & (diff <= n_back)
    has_prev = (jnp.arange(nbk)[:, None, None] > 0) | (kj[None, None, :] >= qbl)
    mask = (band[None] & has_prev)[:, None]
    bias = bias_g[t5_bucket(diff * dil)].transpose(2, 0, 1).astype(jnp.float32)
    logits = jnp.einsum('brnqhd,brnkhd->brnhqk', qb, kk).astype(jnp.float32) * ATTN_SCALE + bias
    p, lse = masked_softmax_lse(logits, mask)
    o = jnp.einsum('brnhqk,brnkhd->brnqhd', p.astype(v.dtype), vv)
    o = o.reshape(b, dil, m, h, dh).transpose(0, 2, 1, 3, 4).reshape(b, sp, h, dh)[:, :s]
    lse = lse.transpose(0, 1, 2, 4, 3).reshape(b, dil, m, h).transpose(0, 2, 1, 3).reshape(b, sp, h)[:, :s]
    return o, lse


def dilated_sample(q, kv_new, buf, win, dil, bias_g):
    t = q.shape[1]
    lb = buf.shape[1]
    n_back = win // dil
    rows = jnp.concatenate([buf, kv_new], axis=1)
    j = jnp.arange(n_back + 1)
    idx = lb + jnp.arange(t)[:, None] - j[None, :] * dil
    valid = (idx >= 0)[None, :, None, :]
    g = rows[:, jnp.maximum(idx, 0)]
    bias = bias_g[t5_bucket(j * dil)].T.astype(jnp.float32)
    logits = jnp.einsum('bthd,btjhd->bthj', q, g[..., 0, :]).astype(jnp.float32) * ATTN_SCALE + bias
    p, lse = masked_softmax_lse(logits, valid)
    o = jnp.einsum('bthj,btjhd->bthd', p.astype(q.dtype), g[..., 1, :])
    return o, lse


def mixer_a(h, w_qkv, qk_gain, w_o, bias_a, bufs):
    b, s, _ = h.shape
    qkv = (h @ w_qkv).reshape(b, s, 3, N_GROUPS_A, H_A, HEAD_DIM)
    q = rms_norm(qkv[:, :, 0], qk_gain[0])
    k = rms_norm(qkv[:, :, 1], qk_gain[1])
    v = qkv[:, :, 2]
    outs, lses, states = [], [], []
    for g, (win, dil) in enumerate(A_GROUPS):
        qg, kg, vg = q[:, :, g], k[:, :, g], v[:, :, g]
        kv = jnp.stack([kg, vg], axis=3)
        if bufs is None:
            o, lse = dilated_prompt(qg, kg, vg, win, dil, bias_a[:, g])
            states.append(kv[:, s - min(win, s):])
        else:
            o, lse = dilated_sample(qg, kv, bufs[g], win, dil, bias_a[:, g])
            states.append(kv)
        outs.append(o)
        lses.append(lse)
    wts = jax.nn.softmax(jnp.stack(lses), axis=0)
    o = jnp.einsum('gbsh,gbshd->bshd', wts, jnp.stack(outs).astype(jnp.float32)).astype(h.dtype)
    return o.reshape(b, s, H_A * HEAD_DIM) @ w_o, states


def moba_select(q, q_pos, kmean):
    nb = kmean.shape[1]
    sc = jnp.einsum('bthd,bnhd->bthn', q.astype(jnp.float32), kmean)
    own = q_pos // MOBA_BLOCK
    past = jnp.arange(nb)[None, :] < own[:, None]
    sc = jnp.where(past[None, :, None, :], sc, NEG_INF)
    _, top = lax.top_k(sc, min(MOBA_TOPK, nb))
    own_b = jnp.broadcast_to(own[None, :, None, None], top.shape[:-1] + (1,)).astype(top.dtype)
    blocks = jnp.concatenate([top, own_b], axis=-1)
    valid = jnp.concatenate([top < own[None, :, None, None], jnp.ones(own_b.shape, bool)], axis=-1)
    return blocks, valid


def moba_prompt(q, k, v, bias_b):
    b, s, h, dh = q.shape
    sp = -(-s // MOBA_BLOCK) * MOBA_BLOCK
    nb = sp // MOBA_BLOCK
    pad = ((0, 0), (0, sp - s), (0, 0), (0, 0))
    kb = jnp.pad(k, pad).reshape(b, nb, MOBA_BLOCK, h, dh)
    vb = jnp.pad(v, pad).reshape(b, nb, MOBA_BLOCK, h, dh)
    kmean = jnp.sum(kb.astype(jnp.float32), axis=2) / MOBA_BLOCK
    kvt = jnp.stack([kb, vb], axis=-2).transpose(0, 3, 1, 2, 4, 5)
    b4 = jnp.arange(b)[:, None, None, None]
    h4 = jnp.arange(h)[None, None, :, None]
    h5 = h4[..., None]
    offs = jnp.arange(MOBA_BLOCK)

    def attend_chunk(args):
        qc, pc = args
        blocks, valid = moba_select(qc, pc, kmean)
        kv = kvt[b4, h4, blocks]
        key_pos = blocks[..., None] * MOBA_BLOCK + offs
        qp = pc[None, :, None, None, None]
        mask = valid[..., None] & (key_pos <= qp)
        logits = (jnp.einsum('bthd,bthsnd->bthsn', qc, kv[..., 0, :]).astype(jnp.float32) * ATTN_SCALE
                  + bias_b[t5_bucket(qp - key_pos), h5].astype(jnp.float32))
        nk = blocks.shape[-1] * MOBA_BLOCK
        p, _ = masked_softmax_lse(logits.reshape(logits.shape[:3] + (nk,)), mask.reshape(mask.shape[:3] + (nk,)))
        return jnp.einsum('bthk,bthkd->bthd', p.astype(qc.dtype), kv[..., 1, :].reshape(kv.shape[:3] + (nk, dh)))

    nc = s // MOBA_Q_CHUNK
    qs = q.reshape(b, nc, MOBA_Q_CHUNK, h, dh).transpose(1, 0, 2, 3, 4)
    ps = jnp.arange(s, dtype=jnp.int32).reshape(nc, MOBA_Q_CHUNK)
    o = lax.map(attend_chunk, (qs, ps))
    return o.transpose(1, 0, 2, 3, 4).reshape(b, s, h, dh)


def moba_sample(q, k_new, v_new, cache_k, cache_v, page_table, bias_b):
    db, t, h, dh = q.shape
    n_pages = page_table.shape[1]
    past_len = n_pages * PAGE_SIZE
    nb = -(-(past_len + t) // MOBA_BLOCK)
    q_pos = past_len + jnp.arange(t, dtype=jnp.int32)
    page_sum = lax.map(lambda pt: jnp.sum(cache_k[pt].astype(jnp.float32), axis=1), page_table)
    blk_ids = jnp.arange(nb)
    page_onehot = ((jnp.arange(n_pages) * PAGE_SIZE // MOBA_BLOCK)[:, None] == blk_ids[None, :]).astype(jnp.float32)
    new_block = q_pos // MOBA_BLOCK
    new_onehot = (new_block[:, None] == blk_ids[None, :]).astype(jnp.float32)
    kmean = (jnp.einsum('pn,bphd->bnhd', page_onehot, page_sum)
             + jnp.einsum('tn,bthd->bnhd', new_onehot, k_new.astype(jnp.float32))) / MOBA_BLOCK
    blocks, valid = moba_select(q, q_pos, kmean)
    key_pos = blocks[..., None] * MOBA_BLOCK + jnp.arange(MOBA_BLOCK)
    qp = q_pos[None, :, None, None, None]
    b5 = jnp.arange(db)[:, None, None, None, None]
    h5 = jnp.arange(h)[None, None, :, None, None]
    phys = page_table[b5, jnp.clip(key_pos // PAGE_SIZE, 0, n_pages - 1)]
    off = key_pos % PAGE_SIZE
    kc = cache_k[phys, off, h5]
    vc = cache_v[phys, off, h5]
    mask_c = valid[..., None] & (key_pos < past_len) & (key_pos <= qp)
    logits_c = (jnp.einsum('bthd,bthsnd->bthsn', q, kc).astype(jnp.float32) * ATTN_SCALE
                + bias_b[t5_bucket(qp - key_pos), h5].astype(jnp.float32))
    nk = blocks.shape[-1] * MOBA_BLOCK
    in_sel = jnp.any((blocks[..., None] == new_block) & valid[..., None], axis=3)
    mask_n = in_sel & (q_pos[:, None] >= q_pos[None, :])[None, :, None, :]
    bias_n = bias_b[t5_bucket(q_pos[:, None] - q_pos[None, :])].transpose(0, 2, 1)[None].astype(jnp.float32)
    logits_n = jnp.einsum('bthd,bjhd->bthj', q, k_new).astype(jnp.float32) * ATTN_SCALE + bias_n
    logits = jnp.concatenate([logits_c.reshape(db, t, h, nk), logits_n], axis=-1)
    mask = jnp.concatenate([mask_c.reshape(db, t, h, nk), mask_n], axis=-1)
    p, _ = masked_softmax_lse(logits, mask)
    p = p.astype(q.dtype)
    return (jnp.einsum('bthk,bthkd->bthd', p[..., :nk], vc.reshape(db, t, h, nk, dh))
            + jnp.einsum('bthj,bjhd->bthd', p[..., nk:], v_new))


def mixer_b(h, w_qkv, qk_gain, w_o, bias_b, cache):
    b, s, _ = h.shape
    qkv = (h @ w_qkv).reshape(b, s, 3, H_B, HEAD_DIM)
    q = rms_norm(qkv[:, :, 0], qk_gain[0])
    k = rms_norm(qkv[:, :, 1], qk_gain[1])
    v = qkv[:, :, 2]
    if cache is None:
        o = moba_prompt(q, k, v, bias_b)
    else:
        o = moba_sample(q, k, v, cache[0], cache[1], cache[2], bias_b)
    return o.reshape(b, s, H_B * HEAD_DIM) @ w_o, k, v


def setup_inputs(seed: int = 0) -> dict:
    key = jax.random.key(seed)
    ks = list(jax.random.split(key, 24))

    def nrm(i, shape, scale=1.0):
        return jax.random.normal(ks[i], shape, jnp.float32) * scale

    n_pages = PAST_LEN // PAGE_SIZE
    n_used = DEC_BATCH * n_pages
    n_phys = n_used + -(-n_used // 4)
    qkv_a = 3 * N_GROUPS_A * H_A * HEAD_DIM
    qkv_b = 3 * H_B * HEAD_DIM
    page_table = jax.random.permutation(ks[9], n_phys)[:n_used].reshape(DEC_BATCH, n_pages).astype(jnp.int32)
    return {
        'x_prompt': nrm(0, (BATCH, SEQ, D_MODEL)),
        'x_sample': nrm(1, (DEC_BATCH, DEC_SEQ, D_MODEL)),
        'c_prompt': nrm(2, (BATCH, D_MODEL)),
        'c_sample': nrm(3, (DEC_BATCH, D_MODEL)),
        'cache_a_w128': nrm(4, (N_A_LAYERS, DEC_BATCH, min(A_GROUPS[0][0], PAST_LEN), H_A, 2, HEAD_DIM)),
        'cache_a_w512': nrm(5, (N_A_LAYERS, DEC_BATCH, min(A_GROUPS[1][0], PAST_LEN), H_A, 2, HEAD_DIM)),
        'cache_a_w2048': nrm(6, (N_A_LAYERS, DEC_BATCH, min(A_GROUPS[2][0], PAST_LEN), H_A, 2, HEAD_DIM)),
        'cache_b_k': nrm(7, (N_B_LAYERS, n_phys, PAGE_SIZE, H_B, HEAD_DIM)),
        'cache_b_v': nrm(8, (N_B_LAYERS, n_phys, PAGE_SIZE, H_B, HEAD_DIM)),
        'page_table': page_table,
        'rel_bias': nrm(10, (N_BUCKETS, N_GROUPS_A * H_A + H_B), 0.2),
        'norm_g': 1.0 + nrm(11, (DEPTH, 3, D_MODEL), 0.05),
        'w_ada': nrm(12, (DEPTH, D_MODEL, 9 * D_MODEL), 0.5 * D_MODEL ** -0.5),
        'b_ada': nrm(13, (DEPTH, 9 * D_MODEL), 0.01),
        'w_ffn_in': nrm(14, (DEPTH, 2, D_MODEL, 2 * D_FF), D_MODEL ** -0.5),
        'w_ffn_out': nrm(15, (DEPTH, 2, D_FF, D_MODEL), D_FF ** -0.5),
        'w_qkv_a': nrm(16, (N_A_LAYERS, D_MODEL, qkv_a), D_MODEL ** -0.5),
        'qk_gain_a': 1.0 + nrm(17, (N_A_LAYERS, 2, HEAD_DIM), 0.05),
        'w_o_a': nrm(18, (N_A_LAYERS, H_A * HEAD_DIM, D_MODEL), (H_A * HEAD_DIM) ** -0.5),
        'w_qkv_b': nrm(19, (N_B_LAYERS, D_MODEL, qkv_b), D_MODEL ** -0.5),
        'qk_gain_b': 1.0 + nrm(20, (N_B_LAYERS, 2, HEAD_DIM), 0.05),
        'w_o_b': nrm(21, (N_B_LAYERS, H_B * HEAD_DIM, D_MODEL), (H_B * HEAD_DIM) ** -0.5),
    }


def reference(x_prompt, x_sample, c_prompt, c_sample, cache_a_w128, cache_a_w512, cache_a_w2048,
              cache_b_k, cache_b_v, page_table, rel_bias, norm_g, w_ada, b_ada, w_ffn_in, w_ffn_out,
              w_qkv_a, qk_gain_a, w_o_a, w_qkv_b, qk_gain_b, w_o_b):
    bias_a = rel_bias[:, :N_GROUPS_A * H_A].reshape(N_BUCKETS, N_GROUPS_A, H_A)
    bias_b = rel_bias[:, N_GROUPS_A * H_A:]
    a_bufs = (cache_a_w128, cache_a_w512, cache_a_w2048)
    xp, xs = x_prompt, x_sample
    a_new_p = [[] for _ in A_GROUPS]
    a_new_s = [[] for _ in A_GROUPS]
    bk_p, bk_s, bv_p, bv_s = [], [], [], []
    for layer in range(DEPTH):
        mp = ada_mod(c_prompt, w_ada[layer], b_ada[layer])
        ms = ada_mod(c_sample, w_ada[layer], b_ada[layer])
        xp = ffn_sublayer(xp, norm_g[layer, 0], mp, 0, w_ffn_in[layer, 0], w_ffn_out[layer, 0])
        xs = ffn_sublayer(xs, norm_g[layer, 0], ms, 0, w_ffn_in[layer, 0], w_ffn_out[layer, 0])
        hp = norm_modulate(xp, norm_g[layer, 1], mp, 1)
        hs = norm_modulate(xs, norm_g[layer, 1], ms, 1)
        i = layer // 2
        if layer % 2 == 0:
            yp, st_p = mixer_a(hp, w_qkv_a[i], qk_gain_a[i], w_o_a[i], bias_a, None)
            ys, st_s = mixer_a(hs, w_qkv_a[i], qk_gain_a[i], w_o_a[i], bias_a, tuple(buf[i] for buf in a_bufs))
            for g in range(N_GROUPS_A):
                a_new_p[g].append(st_p[g])
                a_new_s[g].append(st_s[g])
        else:
            yp, kp, vp = mixer_b(hp, w_qkv_b[i], qk_gain_b[i], w_o_b[i], bias_b, None)
            ys, kn, vn = mixer_b(hs, w_qkv_b[i], qk_gain_b[i], w_o_b[i], bias_b,
                                 (cache_b_k[i], cache_b_v[i], page_table))
            bk_p.append(kp)
            bv_p.append(vp)
            bk_s.append(kn)
            bv_s.append(vn)
        xp = xp + mp[:, 1, 2][:, None, :] * yp
        xs = xs + ms[:, 1, 2][:, None, :] * ys
        xp = ffn_sublayer(xp, norm_g[layer, 2], mp, 2, w_ffn_in[layer, 1], w_ffn_out[layer, 1])
        xs = ffn_sublayer(xs, norm_g[layer, 2], ms, 2, w_ffn_in[layer, 1], w_ffn_out[layer, 1])
    new_a_w128_prompt = jnp.stack(a_new_p[0])
    new_a_w128_sample = jnp.stack(a_new_s[0])
    new_a_w512_prompt = jnp.stack(a_new_p[1])
    new_a_w512_sample = jnp.stack(a_new_s[1])
    new_a_w2048_prompt = jnp.stack(a_new_p[2])
    new_a_w2048_sample = jnp.stack(a_new_s[2])
    new_b_k_prompt = jnp.stack(bk_p)
    new_b_k_sample = jnp.stack(bk_s)
    new_b_v_prompt = jnp.stack(bv_p)
    new_b_v_sample = jnp.stack(bv_s)
    return (xp, xs, new_a_w128_prompt, new_a_w128_sample, new_a_w512_prompt, new_a_w512_sample,
            new_a_w2048_prompt, new_a_w2048_sample, new_b_k_prompt, new_b_k_sample,
            new_b_v_prompt, new_b_v_sample)
```

```python
import functools
import math

import jax
import jax.numpy as jnp
from jax import lax
from jax.experimental import pallas as pl
from jax.experimental.pallas import tpu as pltpu

F32 = jnp.float32
BF16 = jnp.bfloat16

D_MODEL = 2048
HEAD_DIM = 128
A_GROUPS = ((128, 1), (512, 4), (2048, 16))
N_GROUPS_A = 3
H_A = 8
A_Q_BLOCK = 128
H_B = 16
MOBA_BLOCK = 256
MOBA_TOPK = 3
N_BUCKETS = 32
REL_MAX_DIST = 4096
D_FF = 5632
FFN_RES = 0.5
RMS_EPS = 1e-6
NEG_INF = -1e30
ATTN_SCALE = HEAD_DIM ** -0.5
PAGE_SIZE = 128

VMEM_LIMIT_BYTES = 56 * 1024 * 1024
ROW_TILE = 1024

_TRANS_B = (((1,), (1,)), ((), ()))


def _params(*sem):
    return pltpu.CompilerParams(dimension_semantics=sem, vmem_limit_bytes=VMEM_LIMIT_BYTES)


def _t5_bucket(dist):
    n = jnp.maximum(dist, 0)
    max_exact = N_BUCKETS // 2
    nf = jnp.maximum(n, 1).astype(F32)
    large = max_exact + (jnp.log(nf / max_exact) / math.log(REL_MAX_DIST / max_exact)
                         * (N_BUCKETS - max_exact)).astype(jnp.int32)
    large = jnp.minimum(large, N_BUCKETS - 1)
    return jnp.where(n < max_exact, n, large)


def _silu(x):
    return x * jax.nn.sigmoid(x)


def _norm_mod(x, g, scale, shift):
    y = x * lax.rsqrt(jnp.mean(x * x, axis=-1, keepdims=True) + RMS_EPS)
    return (y * g) * (1.0 + scale) + shift


def _ada_kernel(c_ref, w_ref, b_ref, o_ref):
    a = _silu(c_ref[...]).astype(BF16)
    o_ref[...] = jnp.dot(a, w_ref[...].astype(BF16), preferred_element_type=F32) + b_ref[...]


def _ada_all(c_all, w_ada, b_ada):
    depth, d, n = w_ada.shape
    r = c_all.shape[0]
    tn = 1024
    return pl.pallas_call(
        _ada_kernel,
        grid=(depth, n // tn),
        in_specs=[pl.BlockSpec((r, d), lambda l, j: (0, 0)),
                  pl.BlockSpec((None, d, tn), lambda l, j: (l, 0, j)),
                  pl.BlockSpec((None, 1, tn), lambda l, j: (l, 0, j))],
        out_specs=pl.BlockSpec((None, r, tn), lambda l, j: (l, 0, j)),
        out_shape=jax.ShapeDtypeStruct((depth, r, n), F32),
        compiler_params=_params("arbitrary", "arbitrary"),
        name="ada_mod",
    )(c_all, w_ada, b_ada.reshape(depth, 1, n))


def _ffn_in_kernel(x_ref, g_ref, sc_ref, sh_ref, wg_ref, wu_ref, o_ref, h_ref):
    @pl.when(pl.program_id(1) == 0)
    def _():
        h_ref[...] = _norm_mod(x_ref[...], g_ref[...], sc_ref[...], sh_ref[...]).astype(BF16)

    h = h_ref[...]
    a = jnp.dot(h, wg_ref[...].astype(BF16), preferred_element_type=F32)
    u = jnp.dot(h, wu_ref[...].astype(BF16), preferred_element_type=F32)
    o_ref[...] = (_silu(a) * u).astype(BF16)


def _mod_spec(mod, tm, col_block, width):
    if mod.shape[0] == 1:
        return pl.BlockSpec((1, width), lambda i, j: (0, col_block(j)))
    return pl.BlockSpec((tm, width), lambda i, j: (i, col_block(j)))


def _ffn_in(x, g, mod, sub, w_in):
    m, d = x.shape
    f = w_in.shape[1] // 2
    tm = min(m, ROW_TILE)
    tn = 256
    nj = f // tn
    return pl.pallas_call(
        _ffn_in_kernel,
        grid=(m // tm, nj),
        in_specs=[pl.BlockSpec((tm, d), lambda i, j: (i, 0)),
                  pl.BlockSpec((1, d), lambda i, j: (0, 0)),
                  _mod_spec(mod, tm, lambda j: 3 * sub + 1, d),
                  _mod_spec(mod, tm, lambda j: 3 * sub, d),
                  pl.BlockSpec((d, tn), lambda i, j: (0, j)),
                  pl.BlockSpec((d, tn), lambda i, j: (0, j + nj))],
        out_specs=pl.BlockSpec((tm, tn), lambda i, j: (i, j)),
        out_shape=jax.ShapeDtypeStruct((m, f), BF16),
        scratch_shapes=[pltpu.VMEM((tm, d), BF16)],
        compiler_params=_params("arbitrary", "arbitrary"),
        name="ffn_in",
    )(x, g.reshape(1, d), mod, mod, w_in, w_in)


def _mm_res_kernel(a_ref, w_ref, x_ref, gate_ref, o_ref, *, coef):
    acc = jnp.dot(a_ref[...].astype(BF16), w_ref[...].astype(BF16), preferred_element_type=F32)
    o_ref[...] = x_ref[...] + (coef * gate_ref[...]) * acc


def _mm_res(a, w, x, mod, sub, coef):
    m, k = a.shape
    n = w.shape[1]
    tm = min(m, ROW_TILE)
    tn = 256
    per = n // tn
    return pl.pallas_call(
        functools.partial(_mm_res_kernel, coef=coef),
        grid=(m // tm, n // tn),
        in_specs=[pl.BlockSpec((tm, k), lambda i, j: (i, 0)),
                  pl.BlockSpec((k, tn), lambda i, j: (0, j)),
                  pl.BlockSpec((tm, tn), lambda i, j: (i, j)),
                  _mod_spec(mod, tm, lambda j: (3 * sub + 2) * per + j, tn)],
        out_specs=pl.BlockSpec((tm, tn), lambda i, j: (i, j)),
        out_shape=jax.ShapeDtypeStruct((m, n), F32),
        compiler_params=_params("arbitrary", "arbitrary"),
        name="mm_res",
    )(a, w, x, mod)


def _qkv_kernel(x_ref, g_ref, sc_ref, sh_ref, w_ref, gain_ref, o_ref, h_ref, *, tiles_per_part):
    j = pl.program_id(1)

    @pl.when(j == 0)
    def _():
        h_ref[...] = _norm_mod(x_ref[...], g_ref[...], sc_ref[...], sh_ref[...]).astype(BF16)

    acc = jnp.dot(h_ref[...], w_ref[...].astype(BF16), preferred_element_type=F32)
    part = j // tiles_per_part

    @pl.when(part < 2)
    def _():
        gain = gain_ref[pl.ds(part, 1), :]
        for c in range(acc.shape[1] // HEAD_DIM):
            a = acc[:, c * HEAD_DIM:(c + 1) * HEAD_DIM]
            y = a * lax.rsqrt(jnp.mean(a * a, axis=-1, keepdims=True) + RMS_EPS)
            o_ref[:, c * HEAD_DIM:(c + 1) * HEAD_DIM] = y * gain

    @pl.when(part == 2)
    def _():
        o_ref[...] = acc


def _qkv(x, g, mod, w_qkv, qk_gain):
    m, d = x.shape
    n = w_qkv.shape[1]
    tm = min(m, ROW_TILE)
    tn = 512
    return pl.pallas_call(
        functools.partial(_qkv_kernel, tiles_per_part=n // 3 // tn),
        grid=(m // tm, n // tn),
        in_specs=[pl.BlockSpec((tm, d), lambda i, j: (i, 0)),
                  pl.BlockSpec((1, d), lambda i, j: (0, 0)),
                  _mod_spec(mod, tm, lambda j: 4, d),
                  _mod_spec(mod, tm, lambda j: 3, d),
                  pl.BlockSpec((d, tn), lambda i, j: (0, j)),
                  pl.BlockSpec((2, HEAD_DIM), lambda i, j: (0, 0))],
        out_specs=pl.BlockSpec((tm, tn), lambda i, j: (i, j)),
        out_shape=jax.ShapeDtypeStruct((m, n), F32),
        scratch_shapes=[pltpu.VMEM((tm, d), BF16)],
        compiler_params=_params("arbitrary", "arbitrary"),
        name="qkv",
    )(x, g.reshape(1, d), mod, mod, w_qkv, qk_gain)


def _dil_kernel(q_ref, kp_ref, kc_ref, vp_ref, vc_ref, bias_ref, o_ref, lse_ref):
    n = pl.program_id(1)
    qb = A_Q_BLOCK
    col = lax.broadcasted_iota(jnp.int32, (qb, 2 * qb), 1)
    has_prev = (n > 0) | (col >= qb)
    lane = lax.broadcasted_iota(jnp.int32, (qb, HEAD_DIM), 1)
    lse_tile = jnp.zeros((qb, HEAD_DIM), F32)
    for h in range(H_A):
        sl = slice(h * HEAD_DIM, (h + 1) * HEAD_DIM)
        q = q_ref[:, sl].astype(BF16)
        kk = jnp.concatenate([kp_ref[:, sl], kc_ref[:, sl]], axis=0).astype(BF16)
        vv = jnp.concatenate([vp_ref[:, sl], vc_ref[:, sl]], axis=0).astype(BF16)
        s = lax.dot_general(q, kk, _TRANS_B, preferred_element_type=F32) * ATTN_SCALE + bias_ref[h]
        s = jnp.where(has_prev, s, NEG_INF)
        mx = jnp.max(s, axis=-1, keepdims=True)
        e = jnp.exp(s - mx)
        den = jnp.sum(e, axis=-1, keepdims=True)
        p = (e / den).astype(BF16)
        o_ref[:, sl] = jnp.dot(p, vv, preferred_element_type=F32)
        lse_tile = jnp.where(lane == h, mx + jnp.log(den), lse_tile)
    lse_ref[...] = lse_tile


def _dilated_prompt(qkv, g, win, dil, bias_g):
    s, n = qkv.shape
    qb = A_Q_BLOCK
    n_back = win // dil
    hw = H_A * HEAD_DIM
    parts = n // hw
    m = s // dil
    qi = qb + jnp.arange(qb)
    kj = jnp.arange(2 * qb)
    diff = qi[:, None] - kj[None, :]
    band = (diff >= 0) & (diff <= n_back)
    bias = bias_g[_t5_bucket(diff * dil)].transpose(2, 0, 1).astype(F32)
    bias = jnp.where(band[None], bias, NEG_INF)
    view = qkv.reshape(m, dil * n)

    def spec(part, prev):
        if prev:
            return pl.BlockSpec((qb, hw), lambda r, i: (jnp.maximum(i - 1, 0), r * parts + part * N_GROUPS_A + g))
        return pl.BlockSpec((qb, hw), lambda r, i: (i, r * parts + part * N_GROUPS_A + g))

    o, lse = pl.pallas_call(
        _dil_kernel,
        grid=(dil, m // qb),
        in_specs=[spec(0, False), spec(1, True), spec(1, False), spec(2, True), spec(2, False),
                  pl.BlockSpec((H_A, qb, 2 * qb), lambda r, i: (0, 0, 0))],
        out_specs=[pl.BlockSpec((qb, hw), lambda r, i: (i, r)),
                   pl.BlockSpec((qb, HEAD_DIM), lambda r, i: (i, r))],
        out_shape=[jax.ShapeDtypeStruct((m, dil * hw), F32),
                   jax.ShapeDtypeStruct((m, dil * HEAD_DIM), F32)],
        compiler_params=_params("arbitrary", "arbitrary"),
        name=f"dilated_prompt_g{g}",
    )(view, view, view, view, view, bias)
    return o.reshape(s, hw), lse.reshape(s, HEAD_DIM)


def _combine_kernel(o0_ref, o1_ref, o2_ref, l0_ref, l1_ref, l2_ref, out_ref):
    l0, l1, l2 = l0_ref[...], l1_ref[...], l2_ref[...]
    mx = jnp.maximum(jnp.maximum(l0, l1), l2)
    e0, e1, e2 = jnp.exp(l0 - mx), jnp.exp(l1 - mx), jnp.exp(l2 - mx)
    den = e0 + e1 + e2
    w0, w1, w2 = e0 / den, e1 / den, e2 / den
    for h in range(H_A):
        sl = slice(h * HEAD_DIM, (h + 1) * HEAD_DIM)
        out_ref[:, sl] = (w0[:, h:h + 1] * o0_ref[:, sl] + w1[:, h:h + 1] * o1_ref[:, sl]
                          + w2[:, h:h + 1] * o2_ref[:, sl]).astype(out_ref.dtype)


def _combine_groups(outs, lses):
    s, hw = outs[0].shape
    tm = 512
    ospec = pl.BlockSpec((tm, hw), lambda i: (i, 0))
    lspec = pl.BlockSpec((tm, HEAD_DIM), lambda i: (i, 0))
    return pl.pallas_call(
        _combine_kernel,
        grid=(s // tm,),
        in_specs=[ospec, ospec, ospec, lspec, lspec, lspec],
        out_specs=ospec,
        out_shape=jax.ShapeDtypeStruct((s, hw), BF16),
        compiler_params=_params("arbitrary"),
        name="combine_groups",
    )(*outs, *lses)


def _dil_sample_kernel(q_ref, b0_ref, b1_ref, b2_ref, bias_ref, o_ref):
    hw = H_A * HEAD_DIM
    row = q_ref[0]
    outs, lses = [], []
    for g, buf_ref in enumerate((b0_ref, b1_ref, b2_ref)):
        o_g, l_g = [], []
        for h in range(H_A):
            c = g * hw + h * HEAD_DIM
            q = row[:, c:c + HEAD_DIM]
            k_new = row[:, 3 * hw + c:3 * hw + c + HEAD_DIM]
            v_new = row[:, 6 * hw + c:6 * hw + c + HEAD_DIM]
            kb = buf_ref[0, :, 2 * h * HEAD_DIM:(2 * h + 1) * HEAD_DIM]
            vb = buf_ref[0, :, (2 * h + 1) * HEAD_DIM:(2 * h + 2) * HEAD_DIM]
            bias = bias_ref[g * H_A + h]
            nb = kb.shape[0]
            s_buf = jnp.sum(kb * q, axis=-1, keepdims=True) * ATTN_SCALE + bias[:nb]
            s_new = jnp.sum(k_new * q, axis=-1, keepdims=True) * ATTN_SCALE + bias[nb:nb + 1]
            mx = jnp.maximum(jnp.max(s_buf, axis=0, keepdims=True), s_new)
            e_buf = jnp.exp(s_buf - mx)
            e_new = jnp.exp(s_new - mx)
            den = jnp.sum(e_buf, axis=0, keepdims=True) + e_new
            o = (jnp.sum((e_buf / den) * vb, axis=0, keepdims=True) + (e_new / den) * v_new)
            o_g.append(o)
            l_g.append(mx + jnp.log(den))
        outs.append(o_g)
        lses.append(l_g)
    for h in range(H_A):
        l0, l1, l2 = lses[0][h], lses[1][h], lses[2][h]
        mx = jnp.maximum(jnp.maximum(l0, l1), l2)
        e0, e1, e2 = jnp.exp(l0 - mx), jnp.exp(l1 - mx), jnp.exp(l2 - mx)
        den = e0 + e1 + e2
        o_ref[0, :, h * HEAD_DIM:(h + 1) * HEAD_DIM] = (
            (e0 / den) * outs[0][h] + (e1 / den) * outs[1][h] + (e2 / den) * outs[2][h])


def _dilated_sample(qkv_s, bufs, bias_a):
    db, n = qkv_s.shape
    hw = H_A * HEAD_DIM
    views, biases = [], []
    for g, (win, dil) in enumerate(A_GROUPS):
        n_back = win // dil
        lb = bufs[g].shape[1]
        assert lb == win and lb % dil == 0, "window buffer must hold the full window"
        views.append(bufs[g].reshape(db, lb // dil, dil * 2 * hw))
        j = jnp.concatenate([n_back - jnp.arange(n_back), jnp.zeros((8,), jnp.int32)])
        biases.append(bias_a[:, g][_t5_bucket(j * dil)].T.astype(F32))
    bias = jnp.concatenate(biases, axis=0)[:, :, None]
    nb = A_GROUPS[0][0] // A_GROUPS[0][1]
    bspec = pl.BlockSpec((1, nb, 2 * hw), lambda b: (b, 0, 0))
    out = pl.pallas_call(
        _dil_sample_kernel,
        grid=(db,),
        in_specs=[pl.BlockSpec((1, 1, n), lambda b: (b, 0, 0)), bspec, bspec, bspec,
                  pl.BlockSpec(bias.shape, lambda b: (0, 0, 0))],
        out_specs=pl.BlockSpec((1, 1, hw), lambda b: (b, 0, 0)),
        out_shape=jax.ShapeDtypeStruct((db, 1, hw), F32),
        compiler_params=_params("arbitrary"),
        name="dilated_sample",
    )(qkv_s.reshape(db, 1, n), *views, bias)
    return out.reshape(db, hw)


def _kmean_kernel(k_ref, o_ref):
    for b in range(o_ref.shape[0]):
        o_ref[b:b + 1, :] = jnp.sum(k_ref[b * MOBA_BLOCK:(b + 1) * MOBA_BLOCK, :], axis=0,
                                    keepdims=True) / MOBA_BLOCK


def _moba_kmean(qkv):
    s = qkv.shape[0]
    hw = H_B * HEAD_DIM
    per = 8
    nb = s // MOBA_BLOCK
    return pl.pallas_call(
        _kmean_kernel,
        grid=(nb // per,),
        in_specs=[pl.BlockSpec((per * MOBA_BLOCK, hw), lambda i: (i, 1))],
        out_specs=pl.BlockSpec((per, hw), lambda i: (i, 0)),
        out_shape=jax.ShapeDtypeStruct((nb, hw), F32),
        compiler_params=_params("arbitrary"),
        name="moba_kmean",
    )(qkv)


def _top3_mask(sc, own):
    lane = lax.broadcasted_iota(jnp.int32, sc.shape, 1)
    lane_f = lane.astype(F32)
    past = lane < own
    cur = jnp.where(past, sc, NEG_INF)
    sel = lane == own
    for _ in range(MOBA_TOPK):
        mx = jnp.max(cur, axis=-1, keepdims=True)
        idx = jnp.min(jnp.where(cur == mx, lane_f, float(sc.shape[1])), axis=-1, keepdims=True)
        pick = lane_f == idx
        sel = sel | (pick & past)
        cur = jnp.where(pick, -jnp.inf, cur)
    return jnp.where(sel, 0.0, NEG_INF)


def _moba_select_kernel(q_ref, km_ref, o_ref, *, tq):
    sc = lax.dot_general(q_ref[...], km_ref[...], _TRANS_B, preferred_element_type=F32,
                         precision=lax.Precision.HIGHEST)
    pos = pl.program_id(1) * tq + lax.broadcasted_iota(jnp.int32, (tq, 1), 0)
    o_ref[...] = _top3_mask(sc, pos // MOBA_BLOCK)


def _moba_select(qkv, kmean):
    s = qkv.shape[0]
    nb = kmean.shape[0]
    tq = 1024
    km = jnp.pad(kmean, ((0, HEAD_DIM - nb), (0, 0)))
    return pl.pallas_call(
        functools.partial(_moba_select_kernel, tq=tq),
        grid=(H_B, s // tq),
        in_specs=[pl.BlockSpec((tq, HEAD_DIM), lambda h, t: (t, h)),
                  pl.BlockSpec((HEAD_DIM, HEAD_DIM), lambda h, t: (0, h))],
        out_specs=pl.BlockSpec((None, tq, HEAD_DIM), lambda h, t: (h, t, 0)),
        out_shape=jax.ShapeDtypeStruct((H_B, s, HEAD_DIM), F32),
        compiler_params=_params("arbitrary", "arbitrary"),
        name="moba_select",
    )(qkv, km)


def _moba_attn_kernel(q_ref, k_ref, v_ref, sel_ref, bias_ref, o_ref):
    i = pl.program_id(1)
    blk = MOBA_BLOCK
    qb = q_ref[...].astype(BF16)
    sel = sel_ref[...]
    lane = lax.broadcasted_iota(jnp.int32, sel.shape, 1)
    row = lax.broadcasted_iota(jnp.int32, (blk, blk), 0)
    col = lax.broadcasted_iota(jnp.int32, (blk, blk), 1)
    causal = col <= row

    def body(j, carry):
        m, l, acc = carry
        start = pl.multiple_of(j * blk, blk)
        kj = k_ref[pl.ds(start, blk), :].astype(BF16)
        vj = v_ref[pl.ds(start, blk), :].astype(BF16)
        s = lax.dot_general(qb, kj, _TRANS_B, preferred_element_type=F32) * ATTN_SCALE + bias_ref[i - j]
        s = s + jnp.sum(jnp.where(lane == j, sel, 0.0), axis=-1, keepdims=True)
        s = jnp.where((j < i) | causal, s, NEG_INF)
        m_new = jnp.maximum(m, jnp.max(s, axis=-1, keepdims=True))
        alpha = jnp.exp(m - m_new)
        p = jnp.exp(s - m_new)
        l = alpha * l + jnp.sum(p, axis=-1, keepdims=True)
        acc = alpha * acc + jnp.dot(p.astype(BF16), vj, preferred_element_type=F32)
        return m_new, l, acc

    init = (jnp.full((blk, 1), NEG_INF, F32), jnp.zeros((blk, 1), F32), jnp.zeros((blk, HEAD_DIM), F32))
    _, l, acc = lax.fori_loop(0, i + 1, body, init)
    o_ref[...] = (acc / l).astype(o_ref.dtype)


def _moba_prompt(qkv, bias_b):
    s = qkv.shape[0]
    blk = MOBA_BLOCK
    nb = s // blk
    kmean = _moba_kmean(qkv)
    sel = _moba_select(qkv, kmean)
    a = jnp.arange(blk)
    dist = jnp.arange(nb)[:, None, None] * blk + a[None, :, None] - a[None, None, :]
    toep = jnp.take(bias_b.T.astype(F32), _t5_bucket(dist), axis=1)
    return pl.pallas_call(
        _moba_attn_kernel,
        grid=(H_B, nb),
        in_specs=[pl.BlockSpec((blk, HEAD_DIM), lambda h, i: (i, h)),
                  pl.BlockSpec((s, HEAD_DIM), lambda h, i: (0, H_B + h)),
                  pl.BlockSpec((s, HEAD_DIM), lambda h, i: (0, 2 * H_B + h)),
                  pl.BlockSpec((None, blk, HEAD_DIM), lambda h, i: (h, i, 0)),
                  pl.BlockSpec((None, nb, blk, blk), lambda h, i: (h, 0, 0, 0))],
        out_specs=pl.BlockSpec((blk, HEAD_DIM), lambda h, i: (i, h)),
        out_shape=jax.ShapeDtypeStruct((s, H_B * HEAD_DIM), BF16),
        compiler_params=_params("arbitrary", "arbitrary"),
        name="moba_prompt",
    )(qkv, qkv, qkv, sel, toep)


def _page_sum_kernel(pt_ref, p0_ref, p1_ref, o_ref):
    del pt_ref
    blk = pl.program_id(1)
    o_ref[0, pl.ds(blk, 1), :] = (jnp.sum(p0_ref[0], axis=0, keepdims=True)
                                  + jnp.sum(p1_ref[0], axis=0, keepdims=True))


def _moba_block_sums(cache_k, page_table):
    n_phys, page, hw = cache_k.shape
    db, n_pages = page_table.shape
    per = MOBA_BLOCK // page
    assert per == 2
    nb = n_pages // per

    def pspec(which):
        return pl.BlockSpec((1, page, hw), lambda b, n, pt: (pt[b * n_pages + per * n + which], 0, 0))

    return pl.pallas_call(
        _page_sum_kernel,
        grid_spec=pltpu.PrefetchScalarGridSpec(
            num_scalar_prefetch=1,
            grid=(db, nb),
            in_specs=[pspec(0), pspec(1)],
            out_specs=pl.BlockSpec((1, nb, hw), lambda b, n, pt: (b, 0, 0))),
        out_shape=jax.ShapeDtypeStruct((db, nb, hw), F32),
        compiler_params=_params("arbitrary", "arbitrary"),
        name="moba_page_sums",
    )(page_table.reshape(-1), cache_k, cache_k)


def _moba_sample_select_kernel(q_ref, ks_ref, o_ref, *, own):
    q = q_ref[0]
    nb = ks_ref.shape[1]
    lane = lax.broadcasted_iota(jnp.int32, (H_B, HEAD_DIM), 1)
    sc = jnp.zeros((H_B, HEAD_DIM), F32)
    for n in range(nb):
        kmean = ks_ref[0, n] / MOBA_BLOCK
        sc = jnp.where(lane == n, jnp.sum(q * kmean, axis=-1, keepdims=True), sc)
    mask = _top3_mask(sc, jnp.full((H_B, 1), own, jnp.int32))
    lane_f = lane.astype(F32)
    chosen = (mask == 0.0) & (lane < own)
    out = jnp.full((H_B, HEAD_DIM), -1.0, F32)
    cur = jnp.where(chosen, lane_f, float(HEAD_DIM))
    for r in range(MOBA_TOPK):
        idx = jnp.min(cur, axis=-1, keepdims=True)
        out = jnp.where(lane == r, jnp.where(idx < HEAD_DIM, idx, -1.0), out)
        cur = jnp.where(cur == idx, float(HEAD_DIM), cur)
    o_ref[0] = out.astype(jnp.int32)


def _moba_sample_select(q, ksum, own):
    db, nb = ksum.shape[:2]
    return pl.pallas_call(
        functools.partial(_moba_sample_select_kernel, own=own),
        grid=(db,),
        in_specs=[pl.BlockSpec((1, H_B, HEAD_DIM), lambda b: (b, 0, 0)),
                  pl.BlockSpec((1, nb, H_B, HEAD_DIM), lambda b: (b, 0, 0, 0))],
        out_specs=pl.BlockSpec((1, H_B, HEAD_DIM), lambda b: (b, 0, 0)),
        out_shape=jax.ShapeDtypeStruct((db, H_B, HEAD_DIM), jnp.int32),
        compiler_params=_params("arbitrary"),
        name="moba_sample_select",
    )(q, ksum)


def _moba_sample_attn_kernel(pages_ref, lpage_ref, q_ref, kn_ref, vn_ref, bias_ref, bias0_ref, *refs,
                             n_slots):
    k_refs, v_refs, o_ref = refs[:n_slots], refs[n_slots:2 * n_slots], refs[2 * n_slots]
    del pages_ref
    b, h = pl.program_id(0), pl.program_id(1)
    rows = 8
    q = jnp.broadcast_to(q_ref[0, 0], (rows, HEAD_DIM))
    qb = q.astype(BF16)
    s_list = []
    for t in range(n_slots):
        lp = lpage_ref[(b * H_B + h) * n_slots + t]
        s = lax.dot_general(qb, k_refs[t][0].astype(BF16), _TRANS_B, preferred_element_type=F32) * ATTN_SCALE
        s = s + bias_ref[0, pl.ds(jnp.maximum(lp, 0), 1), :]
        s_list.append(jnp.where(lp >= 0, s, NEG_INF))
    s_new = (jnp.sum(q.astype(BF16).astype(F32) * kn_ref[0, 0].astype(BF16).astype(F32), axis=-1, keepdims=True)
             * ATTN_SCALE + bias0_ref[0])
    mx = s_new
    for s in s_list:
        mx = jnp.maximum(mx, jnp.max(s, axis=-1, keepdims=True))
    e_new = jnp.exp(s_new - mx)
    den = e_new
    e_list = []
    for s in s_list:
        e = jnp.exp(s - mx)
        e_list.append(e)
        den = den + jnp.sum(e, axis=-1, keepdims=True)
    p_new = (e_new / den).astype(BF16).astype(F32)
    acc = p_new * vn_ref[0, 0].astype(BF16).astype(F32)
    for t in range(n_slots):
        acc = acc + jnp.dot((e_list[t] / den).astype(BF16), v_refs[t][0].astype(BF16),
                            preferred_element_type=F32)
    o_ref[0, 0] = acc[0:1]


def _moba_sample(qkv_s, cache_k, cache_v, page_table, bias_b):
    db = qkv_s.shape[0]
    n_phys, page = cache_k.shape[:2]
    hw = H_B * HEAD_DIM
    n_pages = page_table.shape[1]
    past_len = n_pages * page
    assert past_len % MOBA_BLOCK == 0 and MOBA_BLOCK % page == 0
    own = past_len // MOBA_BLOCK
    per = MOBA_BLOCK // page
    n_slots = MOBA_TOPK * per
    ck = cache_k.reshape(n_phys, page, hw)
    cv = cache_v.reshape(n_phys, page, hw)
    q = qkv_s[:, :hw].reshape(db, H_B, HEAD_DIM)
    k_new = qkv_s[:, hw:2 * hw].reshape(db, H_B, 1, HEAD_DIM)
    v_new = qkv_s[:, 2 * hw:].reshape(db, H_B, 1, HEAD_DIM)

    ksum = _moba_block_sums(ck, page_table).reshape(db, own, H_B, HEAD_DIM)
    blocks = _moba_sample_select(q, ksum, own)[:, :, :MOBA_TOPK]
    lpage = jnp.where(blocks[..., None] >= 0, blocks[..., None] * per + jnp.arange(per), -1)
    lpage = lpage.reshape(db, H_B, n_slots)
    phys = jnp.take_along_axis(page_table[:, None, :], jnp.maximum(lpage, 0).reshape(db, 1, -1), axis=2)
    phys = phys.reshape(db, H_B, n_slots)
    key_pos = jnp.arange(past_len).reshape(n_pages, page)
    bias_tbl = jnp.take(bias_b.T.astype(F32), _t5_bucket(past_len - key_pos), axis=1)
    bias0 = bias_b[0].astype(F32).reshape(H_B, 1, 1)

    def kv_spec(t):
        return pl.BlockSpec((1, page, HEAD_DIM),
                            lambda b, h, pg, lp: (pg[(b * H_B + h) * n_slots + t], 0, h))

    tok = pl.BlockSpec((1, 1, 1, HEAD_DIM), lambda b, h, pg, lp: (b, h, 0, 0))
    out = pl.pallas_call(
        functools.partial(_moba_sample_attn_kernel, n_slots=n_slots),
        grid_spec=pltpu.PrefetchScalarGridSpec(
            num_scalar_prefetch=2,
            grid=(db, H_B),
            in_specs=[tok, tok, tok,
                      pl.BlockSpec((1, n_pages, page), lambda b, h, pg, lp: (h, 0, 0)),
                      pl.BlockSpec((1, 1, 1), lambda b, h, pg, lp: (h, 0, 0))]
                     + [kv_spec(t) for t in range(n_slots)] * 2,
            out_specs=tok),
        out_shape=jax.ShapeDtypeStruct((db, H_B, 1, HEAD_DIM), F32),
        compiler_params=_params("arbitrary", "arbitrary"),
        name="moba_sample_attn",
    )(phys.reshape(-1), lpage.reshape(-1), q.reshape(db, H_B, 1, HEAD_DIM), k_new, v_new, bias_tbl, bias0,
      *([ck] * n_slots), *([cv] * n_slots))
    return out.reshape(db, hw)


def kernel(x_prompt, x_sample, c_prompt, c_sample, cache_a_w128, cache_a_w512, cache_a_w2048, cache_b_k, cache_b_v, page_table, rel_bias, norm_g, w_ada, b_ada, w_ffn_in, w_ffn_out, w_qkv_a, qk_gain_a, w_o_a, w_qkv_b, qk_gain_b, w_o_b):
    batch, seq, d = x_prompt.shape
    db, dec_seq, _ = x_sample.shape
    assert batch == 1 and dec_seq == 1, "one prompt sequence, one new token per decode sequence"
    depth = norm_g.shape[0]
    n_ab = N_GROUPS_A * H_A
    bias_a = rel_bias[:, :n_ab].reshape(N_BUCKETS, N_GROUPS_A, H_A)
    bias_b = rel_bias[:, n_ab:]
    a_bufs = (cache_a_w128, cache_a_w512, cache_a_w2048)
    hw_a = H_A * HEAD_DIM
    hw_b = H_B * HEAD_DIM

    rows = batch + db
    c_all = jnp.pad(jnp.concatenate([c_prompt, c_sample], axis=0), ((0, -rows % 8), (0, 0)))
    mod_all = _ada_all(c_all, w_ada, b_ada)

    xp = x_prompt.reshape(seq, d)
    xs = x_sample.reshape(db, d)
    a_new_p = [[] for _ in A_GROUPS]
    a_new_s = [[] for _ in A_GROUPS]
    bk_p, bk_s, bv_p, bv_s = [], [], [], []
    for layer in range(depth):
        mp = mod_all[layer, :batch]
        ms = mod_all[layer, batch:rows]
        hp = _ffn_in(xp, norm_g[layer, 0], mp, 0, w_ffn_in[layer, 0])
        xp = _mm_res(hp, w_ffn_out[layer, 0], xp, mp, 0, FFN_RES)
        hs = _ffn_in(xs, norm_g[layer, 0], ms, 0, w_ffn_in[layer, 0])
        xs = _mm_res(hs, w_ffn_out[layer, 0], xs, ms, 0, FFN_RES)
        i = layer // 2
        if layer % 2 == 0:
            qkv_p = _qkv(xp, norm_g[layer, 1], mp, w_qkv_a[i], qk_gain_a[i])
            qkv_s = _qkv(xs, norm_g[layer, 1], ms, w_qkv_a[i], qk_gain_a[i])
            outs, lses = [], []
            for g, (win, dil) in enumerate(A_GROUPS):
                o, lse = _dilated_prompt(qkv_p, g, win, dil, bias_a[:, g])
                outs.append(o)
                lses.append(lse)
            yp = _combine_groups(outs, lses)
            ys = _dilated_sample(qkv_s, tuple(buf[i] for buf in a_bufs), bias_a)
            w_o = w_o_a[i]
            kv_p = qkv_p.reshape(seq, 3, N_GROUPS_A, H_A, HEAD_DIM)
            kv_s = qkv_s.reshape(db, 3, N_GROUPS_A, H_A, HEAD_DIM)
            for g, (win, dil) in enumerate(A_GROUPS):
                keep = min(win, seq)
                a_new_p[g].append(jnp.stack([kv_p[seq - keep:, 1, g], kv_p[seq - keep:, 2, g]], axis=2)[None])
                a_new_s[g].append(jnp.stack([kv_s[:, 1, g], kv_s[:, 2, g]], axis=2)[:, None])
        else:
            qkv_p = _qkv(xp, norm_g[layer, 1], mp, w_qkv_b[i], qk_gain_b[i])
            qkv_s = _qkv(xs, norm_g[layer, 1], ms, w_qkv_b[i], qk_gain_b[i])
            yp = _moba_prompt(qkv_p, bias_b)
            ys = _moba_sample(qkv_s, cache_b_k[i], cache_b_v[i], page_table, bias_b)
            w_o = w_o_b[i]
            bk_p.append(qkv_p[:, hw_b:2 * hw_b].reshape(batch, seq, H_B, HEAD_DIM))
            bv_p.append(qkv_p[:, 2 * hw_b:].reshape(batch, seq, H_B, HEAD_DIM))
            bk_s.append(qkv_s[:, hw_b:2 * hw_b].reshape(db, dec_seq, H_B, HEAD_DIM))
            bv_s.append(qkv_s[:, 2 * hw_b:].reshape(db, dec_seq, H_B, HEAD_DIM))
        xp = _mm_res(yp, w_o, xp, mp, 1, 1.0)
        xs = _mm_res(ys, w_o, xs, ms, 1, 1.0)
        hp = _ffn_in(xp, norm_g[layer, 2], mp, 2, w_ffn_in[layer, 1])
        xp = _mm_res(hp, w_ffn_out[layer, 1], xp, mp, 2, FFN_RES)
        hs = _ffn_in(xs, norm_g[layer, 2], ms, 2, w_ffn_in[layer, 1])
        xs = _mm_res(hs, w_ffn_out[layer, 1], xs, ms, 2, FFN_RES)
    return (xp.reshape(batch, seq, d), xs.reshape(db, dec_seq, d),
            jnp.stack(a_new_p[0]), jnp.stack(a_new_s[0]), jnp.stack(a_new_p[1]), jnp.stack(a_new_s[1]),
            jnp.stack(a_new_p[2]), jnp.stack(a_new_s[2]),
            jnp.stack(bk_p), jnp.stack(bk_s), jnp.stack(bv_p), jnp.stack(bv_s))
```

```python
import functools
import math

import jax
import jax.numpy as jnp
from jax import lax
from jax.experimental import pallas as pl
from jax.experimental.pallas import tpu as pltpu

F32 = jnp.float32
BF16 = jnp.bfloat16

D_MODEL = 2048
HEAD_DIM = 128
A_GROUPS = ((128, 1), (512, 4), (2048, 16))
N_GROUPS_A = 3
H_A = 8
A_Q_BLOCK = 128
H_B = 16
MOBA_BLOCK = 256
MOBA_TOPK = 3
N_BUCKETS = 32
REL_MAX_DIST = 4096
D_FF = 5632
FFN_RES = 0.5
RMS_EPS = 1e-6
NEG_INF = -1e30
ATTN_SCALE = HEAD_DIM ** -0.5
LOG2E = math.log2(math.e)

VMEM_LIMIT_BYTES = 56 * 1024 * 1024
ROW_TILE = 1024
A_SUPER = max(dil for _, dil in A_GROUPS) * A_Q_BLOCK
MOBA_STREAMS = 4

_TRANS_B = (((1,), (1,)), ((), ()))


def _params(*sem):
    return pltpu.CompilerParams(dimension_semantics=sem, vmem_limit_bytes=VMEM_LIMIT_BYTES)


def _t5_bucket(dist):
    n = jnp.maximum(dist, 0)
    max_exact = N_BUCKETS // 2
    nf = jnp.maximum(n, 1).astype(F32)
    large = max_exact + (jnp.log(nf / max_exact) / math.log(REL_MAX_DIST / max_exact)
                         * (N_BUCKETS - max_exact)).astype(jnp.int32)
    large = jnp.minimum(large, N_BUCKETS - 1)
    return jnp.where(n < max_exact, n, large)


def _bias_lookup(table, dist):
    onehot = (_t5_bucket(dist)[..., None] == jnp.arange(N_BUCKETS)).astype(F32)
    return jnp.einsum('...b,bh->...h', onehot, table.astype(F32), precision=lax.Precision.HIGHEST)


def _silu(x):
    return x * jax.nn.sigmoid(x)


def _norm_mod(x, g, scale, shift):
    y = x * lax.rsqrt(jnp.mean(x * x, axis=-1, keepdims=True) + RMS_EPS)
    return (y * g) * (1.0 + scale) + shift


def _ada_kernel(c_ref, w_ref, b_ref, o_ref):
    a = _silu(c_ref[...]).astype(BF16)
    o_ref[...] = jnp.dot(a, w_ref[...].astype(BF16), preferred_element_type=F32) + b_ref[...]


def _ada_all(c_all, w_ada, b_ada):
    depth, d, n = w_ada.shape
    r = c_all.shape[0]
    tn = 1024
    return pl.pallas_call(
        _ada_kernel,
        grid=(depth, n // tn),
        in_specs=[pl.BlockSpec((r, d), lambda l, j: (0, 0)),
                  pl.BlockSpec((None, d, tn), lambda l, j: (l, 0, j)),
                  pl.BlockSpec((None, 1, tn), lambda l, j: (l, 0, j))],
        out_specs=pl.BlockSpec((None, r, tn), lambda l, j: (l, 0, j)),
        out_shape=jax.ShapeDtypeStruct((depth, r, n), F32),
        compiler_params=_params("arbitrary", "arbitrary"),
        name="ada_mod",
    )(c_all, w_ada, b_ada.reshape(depth, 1, n))


def _ffn_in_kernel(x_ref, g_ref, sc_ref, sh_ref, wg_ref, wu_ref, o_ref, h_ref):
    @pl.when(pl.program_id(1) == 0)
    def _():
        h_ref[...] = _norm_mod(x_ref[...], g_ref[...], sc_ref[...], sh_ref[...]).astype(BF16)

    h = h_ref[...]
    a = jnp.dot(h, wg_ref[...].astype(BF16), preferred_element_type=F32)
    u = jnp.dot(h, wu_ref[...].astype(BF16), preferred_element_type=F32)
    o_ref[...] = (_silu(a) * u).astype(BF16)


def _mod_spec(mod, tm, col_block, width):
    if mod.shape[0] == 1:
        return pl.BlockSpec((1, width), lambda i, j: (0, col_block(j)))
    return pl.BlockSpec((tm, width), lambda i, j: (i, col_block(j)))


def _ffn_in(x, g, mod, sub, w_in):
    m, d = x.shape
    f = w_in.shape[1] // 2
    tm = min(m, ROW_TILE)
    tn = 256
    nj = f // tn
    return pl.pallas_call(
        _ffn_in_kernel,
        grid=(m // tm, nj),
        in_specs=[pl.BlockSpec((tm, d), lambda i, j: (i, 0)),
                  pl.BlockSpec((1, d), lambda i, j: (0, 0)),
                  _mod_spec(mod, tm, lambda j: 3 * sub + 1, d),
                  _mod_spec(mod, tm, lambda j: 3 * sub, d),
                  pl.BlockSpec((d, tn), lambda i, j: (0, j)),
                  pl.BlockSpec((d, tn), lambda i, j: (0, j + nj))],
        out_specs=pl.BlockSpec((tm, tn), lambda i, j: (i, j)),
        out_shape=jax.ShapeDtypeStruct((m, f), BF16),
        scratch_shapes=[pltpu.VMEM((tm, d), BF16)],
        compiler_params=_params("arbitrary", "arbitrary"),
        name="ffn_in",
    )(x, g.reshape(1, d), mod, mod, w_in, w_in)


def _mm_res_kernel(a_ref, w_ref, x_ref, gate_ref, o_ref, *, coef):
    acc = jnp.dot(a_ref[...].astype(BF16), w_ref[...].astype(BF16), preferred_element_type=F32)
    o_ref[...] = x_ref[...] + (coef * gate_ref[...]) * acc


def _mm_res(a, w, x, mod, sub, coef):
    m, k = a.shape
    n = w.shape[1]
    tm = min(m, ROW_TILE)
    tn = 256
    per = n // tn
    return pl.pallas_call(
        functools.partial(_mm_res_kernel, coef=coef),
        grid=(m // tm, n // tn),
        in_specs=[pl.BlockSpec((tm, k), lambda i, j: (i, 0)),
                  pl.BlockSpec((k, tn), lambda i, j: (0, j)),
                  pl.BlockSpec((tm, tn), lambda i, j: (i, j)),
                  _mod_spec(mod, tm, lambda j: (3 * sub + 2) * per + j, tn)],
        out_specs=pl.BlockSpec((tm, tn), lambda i, j: (i, j)),
        out_shape=jax.ShapeDtypeStruct((m, n), F32),
        compiler_params=_params("arbitrary", "arbitrary"),
        name="mm_res",
    )(a, w, x, mod)


def _qkv_kernel(x_ref, g_ref, sc_ref, sh_ref, w_ref, gain_ref, o_ref, *rest, tiles_per_part):
    h_ref = rest[-1]
    j = pl.program_id(1)

    @pl.when(j == 0)
    def _():
        h_ref[...] = _norm_mod(x_ref[...], g_ref[...], sc_ref[...], sh_ref[...]).astype(BF16)

    acc = jnp.dot(h_ref[...], w_ref[...].astype(BF16), preferred_element_type=F32)
    part = j // tiles_per_part

    @pl.when(part < 2)
    def _():
        gain = gain_ref[pl.ds(part, 1), :]
        for c in range(acc.shape[1] // HEAD_DIM):
            sl = slice(c * HEAD_DIM, (c + 1) * HEAD_DIM)
            a = acc[:, sl]
            y = (a * lax.rsqrt(jnp.mean(a * a, axis=-1, keepdims=True) + RMS_EPS)) * gain
            o_ref[:, sl] = y
            if len(rest) == 2:
                rest[0][:, sl] = y.astype(BF16)

    @pl.when(part == 2)
    def _():
        o_ref[...] = acc
        if len(rest) == 2:
            rest[0][...] = acc.astype(BF16)


def _qkv(x, g, mod, w_qkv, qk_gain, with_bf16=False):
    m, d = x.shape
    n = w_qkv.shape[1]
    tm = min(m, ROW_TILE)
    tn = 512
    ospec = pl.BlockSpec((tm, tn), lambda i, j: (i, j))
    out_shape = [jax.ShapeDtypeStruct((m, n), F32)]
    if with_bf16:
        out_shape.append(jax.ShapeDtypeStruct((m, n), BF16))
    return pl.pallas_call(
        functools.partial(_qkv_kernel, tiles_per_part=n // 3 // tn),
        grid=(m // tm, n // tn),
        in_specs=[pl.BlockSpec((tm, d), lambda i, j: (i, 0)),
                  pl.BlockSpec((1, d), lambda i, j: (0, 0)),
                  _mod_spec(mod, tm, lambda j: 4, d),
                  _mod_spec(mod, tm, lambda j: 3, d),
                  pl.BlockSpec((d, tn), lambda i, j: (0, j)),
                  pl.BlockSpec((2, HEAD_DIM), lambda i, j: (0, 0))],
        out_specs=[ospec] * len(out_shape),
        out_shape=out_shape,
        scratch_shapes=[pltpu.VMEM((tm, d), BF16)],
        compiler_params=_params("arbitrary", "arbitrary"),
        name="qkv",
    )(x, g.reshape(1, d), mod, mod, w_qkv, qk_gain)


def _dil_kernel(*refs):
    n_in = 5 * N_GROUPS_A
    bias_ref, o_ref, og_ref, lg_ref = refs[n_in:]
    qb = A_Q_BLOCK
    sb = pl.program_id(0)
    col = lax.broadcasted_iota(jnp.int32, (qb, 2 * qb), 1)
    prev_exists = (sb > 0) | (col >= qb)
    for g, (_, dil) in enumerate(A_GROUPS):
        q_ref, kc_ref, kp_ref, vc_ref, vp_ref = refs[5 * g:5 * g + 5]
        nblk = A_SUPER // (dil * qb)
        bias = bias_ref[g]

        def rows_of(n, r):
            return pl.ds(n * qb * dil + r, qb, stride=dil) if dil > 1 else pl.ds(n * qb, qb)

        for r in range(dil):
            for n in range(nblk):
                rows = rows_of(n, r)
                if n > 0:
                    k_prev, v_prev = kc_ref[rows_of(n - 1, r), :], vc_ref[rows_of(n - 1, r), :]
                else:
                    k_prev, v_prev = kp_ref[rows_of(nblk - 1, r), :], vp_ref[rows_of(nblk - 1, r), :]
                q = q_ref[rows, :].astype(BF16)
                kk = jnp.concatenate([k_prev, kc_ref[rows, :]], axis=0).astype(BF16)
                vv = jnp.concatenate([v_prev, vc_ref[rows, :]], axis=0).astype(BF16)
                s = lax.dot_general(q, kk, _TRANS_B, preferred_element_type=F32) * ATTN_SCALE + bias
                if n == 0:
                    s = jnp.where(prev_exists, s, NEG_INF)
                mx = jnp.max(s, axis=-1, keepdims=True)
                e = jnp.exp(s - mx)
                den = jnp.sum(e, axis=-1, keepdims=True)
                og_ref[g, rows, :] = jnp.dot((e / den).astype(BF16), vv, preferred_element_type=F32)
                lg_ref[g, rows, :] = mx + jnp.log(den)
    l0, l1, l2 = lg_ref[0], lg_ref[1], lg_ref[2]
    mx = jnp.maximum(jnp.maximum(l0, l1), l2)
    e0, e1, e2 = jnp.exp(l0 - mx), jnp.exp(l1 - mx), jnp.exp(l2 - mx)
    den = e0 + e1 + e2
    o_ref[...] = ((e0 / den) * og_ref[0] + (e1 / den) * og_ref[1] + (e2 / den) * og_ref[2]).astype(o_ref.dtype)


def _dilated_prompt(qkv, bias_a):
    s, n = qkv.shape
    qb = A_Q_BLOCK
    assert s % A_SUPER == 0
    gh = N_GROUPS_A * H_A
    qi = qb + jnp.arange(qb)
    kj = jnp.arange(2 * qb)
    diff = qi[:, None] - kj[None, :]
    tables = []
    for g, (win, dil) in enumerate(A_GROUPS):
        band = (diff >= 0) & (diff <= win // dil)
        bias = _bias_lookup(bias_a[:, g], diff * dil).transpose(2, 0, 1)
        tables.append(jnp.where(band[None], bias, NEG_INF))
    bias = jnp.stack(tables)

    in_specs, operands = [], []
    for g in range(N_GROUPS_A):
        for part, prev in ((0, False), (1, False), (1, True), (2, False), (2, True)):
            cb = part * gh + g * H_A
            if prev:
                in_specs.append(pl.BlockSpec((A_SUPER, HEAD_DIM),
                                             lambda sb, h, cb=cb: (jnp.maximum(sb - 1, 0), cb + h)))
            else:
                in_specs.append(pl.BlockSpec((A_SUPER, HEAD_DIM), lambda sb, h, cb=cb: (sb, cb + h)))
            operands.append(qkv)
    in_specs.append(pl.BlockSpec((N_GROUPS_A, None, qb, 2 * qb), lambda sb, h: (0, h, 0, 0)))
    return pl.pallas_call(
        _dil_kernel,
        grid=(s // A_SUPER, H_A),
        in_specs=in_specs,
        out_specs=pl.BlockSpec((A_SUPER, HEAD_DIM), lambda sb, h: (sb, h)),
        out_shape=jax.ShapeDtypeStruct((s, H_A * HEAD_DIM), BF16),
        scratch_shapes=[pltpu.VMEM((N_GROUPS_A, A_SUPER, HEAD_DIM), F32),
                        pltpu.VMEM((N_GROUPS_A, A_SUPER, 1), F32)],
        compiler_params=_params("arbitrary", "arbitrary"),
        name="dilated_prompt",
    )(*operands, bias)


def _dil_sample_kernel(q_ref, b0_ref, b1_ref, b2_ref, bias_ref, o_ref):
    hw = H_A * HEAD_DIM
    row = q_ref[0]
    outs, lses = [], []
    for g, buf_ref in enumerate((b0_ref, b1_ref, b2_ref)):
        o_g, l_g = [], []
        for h in range(H_A):
            c = g * hw + h * HEAD_DIM
            q = row[:, c:c + HEAD_DIM]
            k_new = row[:, 3 * hw + c:3 * hw + c + HEAD_DIM]
            v_new = row[:, 6 * hw + c:6 * hw + c + HEAD_DIM]
            kb = buf_ref[:, h, 0, :]
            vb = buf_ref[:, h, 1, :]
            bias = bias_ref[g * H_A + h]
            nb = kb.shape[0]
            s_buf = jnp.sum(kb * q, axis=-1, keepdims=True) * ATTN_SCALE + bias[:nb]
            s_new = jnp.sum(k_new * q, axis=-1, keepdims=True) * ATTN_SCALE + bias[nb:nb + 1]
            mx = jnp.maximum(jnp.max(s_buf, axis=0, keepdims=True), s_new)
            e_buf = jnp.exp(s_buf - mx)
            e_new = jnp.exp(s_new - mx)
            den = jnp.sum(e_buf, axis=0, keepdims=True) + e_new
            o = (jnp.sum((e_buf / den) * vb, axis=0, keepdims=True) + (e_new / den) * v_new)
            o_g.append(o)
            l_g.append(mx + jnp.log(den))
        outs.append(o_g)
        lses.append(l_g)
    for h in range(H_A):
        l0, l1, l2 = lses[0][h], lses[1][h], lses[2][h]
        mx = jnp.maximum(jnp.maximum(l0, l1), l2)
        e0, e1, e2 = jnp.exp(l0 - mx), jnp.exp(l1 - mx), jnp.exp(l2 - mx)
        den = e0 + e1 + e2
        o_ref[0, :, h * HEAD_DIM:(h + 1) * HEAD_DIM] = (
            (e0 / den) * outs[0][h] + (e1 / den) * outs[1][h] + (e2 / den) * outs[2][h])


def _dilated_sample(qkv_s, bufs, layer, bias_a):
    db, n = qkv_s.shape
    hw = H_A * HEAD_DIM
    views, biases, specs = [], [], []
    for g, (win, dil) in enumerate(A_GROUPS):
        n_back = win // dil
        n_layers, _, lb = bufs[g].shape[:3]
        assert lb == win and lb % dil == 0, "window buffer must hold the full window"
        views.append(bufs[g].reshape(n_layers * db, n_back, dil, H_A, 2, HEAD_DIM))
        specs.append(pl.BlockSpec((None, n_back, None, H_A, 2, HEAD_DIM),
                                  lambda b: (layer * db + b, 0, 0, 0, 0, 0)))
        j = jnp.concatenate([n_back - jnp.arange(n_back), jnp.zeros((8,), jnp.int32)])
        biases.append(_bias_lookup(bias_a[:, g], j * dil).T)
    bias = jnp.concatenate(biases, axis=0)[:, :, None]
    out = pl.pallas_call(
        _dil_sample_kernel,
        grid=(db,),
        in_specs=[pl.BlockSpec((1, 1, n), lambda b: (b, 0, 0))] + specs
                 + [pl.BlockSpec(bias.shape, lambda b: (0, 0, 0))],
        out_specs=pl.BlockSpec((1, 1, hw), lambda b: (b, 0, 0)),
        out_shape=jax.ShapeDtypeStruct((db, 1, hw), F32),
        compiler_params=_params("arbitrary"),
        name="dilated_sample",
    )(qkv_s.reshape(db, 1, n), *views, bias)
    return out.reshape(db, hw)


def _kmean_kernel(k_ref, o_ref):
    for b in range(o_ref.shape[0]):
        o_ref[b:b + 1, :] = jnp.sum(k_ref[b * MOBA_BLOCK:(b + 1) * MOBA_BLOCK, :], axis=0,
                                    keepdims=True) / MOBA_BLOCK


def _moba_kmean(qkv):
    s = qkv.shape[0]
    hw = H_B * HEAD_DIM
    per = 8
    nb = s // MOBA_BLOCK
    return pl.pallas_call(
        _kmean_kernel,
        grid=(nb // per,),
        in_specs=[pl.BlockSpec((per * MOBA_BLOCK, hw), lambda i: (i, 1))],
        out_specs=pl.BlockSpec((per, hw), lambda i: (i, 0)),
        out_shape=jax.ShapeDtypeStruct((nb, hw), F32),
        compiler_params=_params("arbitrary"),
        name="moba_kmean",
    )(qkv)


def _top3_mask(sc, own):
    lane = lax.broadcasted_iota(jnp.int32, sc.shape, 1)
    lane_f = lane.astype(F32)
    past = lane < own
    cur = jnp.where(past, sc, NEG_INF)
    sel = lane == own
    for _ in range(MOBA_TOPK):
        mx = jnp.max(cur, axis=-1, keepdims=True)
        idx = jnp.min(jnp.where(cur == mx, lane_f, float(sc.shape[1])), axis=-1, keepdims=True)
        pick = lane_f == idx
        sel = sel | (pick & past)
        cur = jnp.where(pick, -jnp.inf, cur)
    return jnp.where(sel, 0.0, NEG_INF)


def _moba_select_kernel(q_ref, km_ref, o_ref, *, tq):
    q = q_ref[...]
    sc = lax.dot_general(q, km_ref[...], _TRANS_B, preferred_element_type=F32,
                         precision=lax.Precision.HIGHEST)
    pos = pl.program_id(1) * tq + lax.broadcasted_iota(jnp.int32, (tq, 1), 0)
    o_ref[:, :HEAD_DIM] = q.astype(BF16)
    o_ref[:, HEAD_DIM:] = _top3_mask(sc, pos // MOBA_BLOCK).astype(BF16)


def _moba_select(qkv, kmean):
    s = qkv.shape[0]
    nb = kmean.shape[0]
    assert nb <= HEAD_DIM
    tq = 1024
    km = jnp.pad(kmean, ((0, HEAD_DIM - nb), (0, 0)))
    return pl.pallas_call(
        functools.partial(_moba_select_kernel, tq=tq),
        grid=(H_B, s // tq),
        in_specs=[pl.BlockSpec((tq, HEAD_DIM), lambda h, t: (t, h)),
                  pl.BlockSpec((HEAD_DIM, HEAD_DIM), lambda h, t: (0, h))],
        out_specs=pl.BlockSpec((None, tq, 2 * HEAD_DIM), lambda h, t: (h, t, 0)),
        out_shape=jax.ShapeDtypeStruct((H_B, s, 2 * HEAD_DIM), BF16),
        compiler_params=_params("arbitrary", "arbitrary"),
        name="moba_select",
    )(qkv, km)


def _moba_attn_kernel(qx_ref, k_ref, v_ref, rev_ref, o_ref, kx_ref, t_ref, acc_ref, m_ref, l_ref, *, nb):
    d = pl.program_id(1)
    blk = MOBA_BLOCK
    c1 = ATTN_SCALE * LOG2E
    lane = lax.broadcasted_iota(jnp.int32, (blk, HEAD_DIM), 1)

    @pl.when(d == 0)
    def _():
        def fill(b, carry):
            rows = pl.ds(pl.multiple_of(b * blk, blk), blk)
            kx_ref[rows, :HEAD_DIM] = k_ref[rows, :]
            kx_ref[rows, HEAD_DIM:] = jnp.where(lane == b, 1.0, 0.0).astype(BF16)
            return carry
        lax.fori_loop(0, nb, fill, 0)

    vec = jnp.concatenate([rev_ref[pl.ds(nb - d, 1), :], rev_ref[pl.ds(nb + 1 - d, 1), :]], axis=1)
    t = pltpu.roll(jnp.broadcast_to(vec, (blk, 2 * blk)), 0, 1, stride=1, stride_axis=0)[:, blk:]
    row = lax.broadcasted_iota(jnp.int32, (blk, blk), 0)
    col = lax.broadcasted_iota(jnp.int32, (blk, blk), 1)
    t_ref[...] = jnp.where((d > 0) | (col <= row), t, NEG_INF)

    def scores(i):
        qrows = pl.ds(pl.multiple_of(i * blk, blk), blk)
        krows = pl.ds(pl.multiple_of((i - d) * blk, blk), blk)
        s = lax.dot_general(qx_ref[qrows, :], kx_ref[krows, :], _TRANS_B, preferred_element_type=F32)
        return qrows, krows, s * c1 + t_ref[...]

    def first_tiles(ids):
        for i in ids:
            qrows, krows, s = scores(i)
            m = jnp.max(s, axis=-1, keepdims=True)
            p = jnp.exp2(s - m)
            m_ref[qrows, :] = jnp.broadcast_to(m, (blk, HEAD_DIM))
            l_ref[qrows, :] = jnp.broadcast_to(jnp.sum(p, axis=-1, keepdims=True), (blk, HEAD_DIM))
            acc_ref[qrows, :] = jnp.dot(p.astype(BF16), v_ref[krows, :], preferred_element_type=F32)

    def later_tiles(ids):
        loaded = []
        for i in ids:
            qrows, krows, s = scores(i)
            loaded.append((qrows, krows, s, m_ref[qrows, :], l_ref[qrows, :], acc_ref[qrows, :]))
        results = []
        for qrows, krows, s, m_old, l_old, acc_old in loaded:
            m_new = jnp.maximum(m_old, jnp.max(s, axis=-1, keepdims=True))
            alpha = jnp.exp2(m_old - m_new)
            p = jnp.exp2(s - jnp.concatenate([m_new] * (blk // HEAD_DIM), axis=1))
            l_new = alpha * l_old + jnp.sum(p, axis=-1, keepdims=True)
            acc_new = alpha * acc_old + jnp.dot(p.astype(BF16), v_ref[krows, :], preferred_element_type=F32)
            results.append((qrows, m_new, l_new, acc_new))
        for qrows, m_new, l_new, acc_new in results:
            m_ref[qrows, :] = m_new
            l_ref[qrows, :] = l_new
            acc_ref[qrows, :] = acc_new

    u = MOBA_STREAMS
    n_main = (nb - d) // u

    def run(tiles):
        def main_body(it, carry):
            tiles([d + it * u + k for k in range(u)])
            return carry
        lax.fori_loop(0, n_main, main_body, 0)

        def tail_body(it, carry):
            tiles([d + n_main * u + it])
            return carry
        lax.fori_loop(0, (nb - d) - n_main * u, tail_body, 0)

    @pl.when(d == 0)
    def _():
        run(first_tiles)

    @pl.when(d > 0)
    def _():
        run(later_tiles)

    @pl.when(d == nb - 1)
    def _():
        def finish(b, carry):
            rows = pl.ds(pl.multiple_of(b * blk, blk), blk)
            o_ref[rows, :] = (acc_ref[rows, :] / l_ref[rows, :]).astype(o_ref.dtype)
            return carry
        lax.fori_loop(0, nb, finish, 0)


def _moba_prompt(qkv, qkv_bf16, bias_b):
    s = qkv.shape[0]
    blk = MOBA_BLOCK
    nb = s // blk
    kmean = _moba_kmean(qkv)
    qx = _moba_select(qkv, kmean)
    dist = (nb + 1) * blk - jnp.arange((nb + 2) * blk)
    rev = (_bias_lookup(bias_b, dist) * LOG2E).T.reshape(H_B, nb + 2, blk)
    return pl.pallas_call(
        functools.partial(_moba_attn_kernel, nb=nb),
        grid=(H_B, nb),
        in_specs=[pl.BlockSpec((None, s, 2 * HEAD_DIM), lambda h, d: (h, 0, 0)),
                  pl.BlockSpec((s, HEAD_DIM), lambda h, d: (0, H_B + h)),
                  pl.BlockSpec((s, HEAD_DIM), lambda h, d: (0, 2 * H_B + h)),
                  pl.BlockSpec((None, nb + 2, blk), lambda h, d: (h, 0, 0))],
        out_specs=pl.BlockSpec((s, HEAD_DIM), lambda h, d: (0, h)),
        out_shape=jax.ShapeDtypeStruct((s, H_B * HEAD_DIM), BF16),
        scratch_shapes=[pltpu.VMEM((s, 2 * HEAD_DIM), BF16),
                        pltpu.VMEM((blk, blk), F32),
                        pltpu.VMEM((s, HEAD_DIM), F32),
                        pltpu.VMEM((s, HEAD_DIM), F32),
                        pltpu.VMEM((s, HEAD_DIM), F32)],
        compiler_params=_params("arbitrary", "arbitrary"),
        name="moba_prompt",
    )(qx, qkv_bf16, qkv_bf16, rev)


def _page_sum_kernel(pt_ref, *refs, per):
    del pt_ref
    page_refs, o_ref = refs[:-1], refs[-1]
    step = pl.program_id(1)
    for t in range(len(page_refs) // per):
        acc = jnp.sum(page_refs[per * t][0], axis=0)
        for u in range(1, per):
            acc = acc + jnp.sum(page_refs[per * t + u][0], axis=0)
        o_ref[0, pl.ds(step * (len(page_refs) // per) + t, 1)] = acc[None]


def _moba_block_sums(cache_k, layer, page_table):
    n_layers, n_phys, page = cache_k.shape[:3]
    db, n_pages = page_table.shape
    per = MOBA_BLOCK // page
    nb = n_pages // per
    pages_per_step = 8
    assert n_pages % pages_per_step == 0 and pages_per_step % per == 0
    ck = cache_k.reshape(n_layers * n_phys, page, H_B, HEAD_DIM)

    def pspec(t):
        return pl.BlockSpec((1, page, H_B, HEAD_DIM),
                            lambda b, n, pt: (layer * n_phys + pt[b * n_pages + pages_per_step * n + t], 0, 0, 0))

    return pl.pallas_call(
        functools.partial(_page_sum_kernel, per=per),
        grid_spec=pltpu.PrefetchScalarGridSpec(
            num_scalar_prefetch=1,
            grid=(db, n_pages // pages_per_step),
            in_specs=[pspec(t) for t in range(pages_per_step)],
            out_specs=pl.BlockSpec((1, nb, H_B, HEAD_DIM), lambda b, n, pt: (b, 0, 0, 0))),
        out_shape=jax.ShapeDtypeStruct((db, nb, H_B, HEAD_DIM), F32),
        compiler_params=_params("arbitrary", "arbitrary"),
        name="moba_page_sums",
    )(page_table.reshape(-1), *([ck] * pages_per_step))


def _moba_sample_select_kernel(q_ref, ks_ref, o_ref, *, own):
    q = q_ref[0]
    nb = ks_ref.shape[1]
    lane = lax.broadcasted_iota(jnp.int32, (H_B, HEAD_DIM), 1)
    sc = jnp.zeros((H_B, HEAD_DIM), F32)
    for n in range(nb):
        kmean = ks_ref[0, n] / MOBA_BLOCK
        sc = jnp.where(lane == n, jnp.sum(q * kmean, axis=-1, keepdims=True), sc)
    mask = _top3_mask(sc, jnp.full((H_B, 1), own, jnp.int32))
    lane_f = lane.astype(F32)
    chosen = (mask == 0.0) & (lane < own)
    out = jnp.full((H_B, HEAD_DIM), -1.0, F32)
    cur = jnp.where(chosen, lane_f, float(HEAD_DIM))
    for r in range(MOBA_TOPK):
        idx = jnp.min(cur, axis=-1, keepdims=True)
        out = jnp.where(lane == r, jnp.where(idx < HEAD_DIM, idx, -1.0), out)
        cur = jnp.where(cur == idx, float(HEAD_DIM), cur)
    o_ref[0] = out.astype(jnp.int32)


def _moba_sample_select(q, ksum, own):
    db, nb = ksum.shape[:2]
    return pl.pallas_call(
        functools.partial(_moba_sample_select_kernel, own=own),
        grid=(db,),
        in_specs=[pl.BlockSpec((1, H_B, HEAD_DIM), lambda b: (b, 0, 0)),
                  pl.BlockSpec((1, nb, H_B, HEAD_DIM), lambda b: (b, 0, 0, 0))],
        out_specs=pl.BlockSpec((1, H_B, HEAD_DIM), lambda b: (b, 0, 0)),
        out_shape=jax.ShapeDtypeStruct((db, H_B, HEAD_DIM), jnp.int32),
        compiler_params=_params("arbitrary"),
        name="moba_sample_select",
    )(q, ksum)


def _moba_sample_attn_kernel(pages_ref, lpage_ref, q_ref, kn_ref, vn_ref, bias_ref, bias0_ref, k_hbm, v_hbm,
                             o_ref, kbuf, vbuf, sem, *, layer, n_slots):
    b = pl.program_id(0)
    slot = b % 2
    rows = 8

    def copies(seq, buf):
        out = []
        for h in range(H_B):
            for t in range(n_slots):
                pg = pages_ref[(seq * H_B + h) * n_slots + t]
                out.append(pltpu.make_async_copy(k_hbm.at[layer, pg, :, h, :], kbuf.at[buf, h, t], sem.at[buf]))
                out.append(pltpu.make_async_copy(v_hbm.at[layer, pg, :, h, :], vbuf.at[buf, h, t], sem.at[buf]))
        return out

    @pl.when(b == 0)
    def _():
        for cp in copies(0, 0):
            cp.start()

    @pl.when(b + 1 < pl.num_programs(0))
    def _():
        for cp in copies(b + 1, 1 - slot):
            cp.start()

    for cp in copies(b, slot):
        cp.wait()
    for h in range(H_B):
        q = jnp.broadcast_to(q_ref[0, h:h + 1, :], (rows, HEAD_DIM))
        qb = q.astype(BF16)
        s_list = []
        for t in range(n_slots):
            lp = lpage_ref[(b * H_B + h) * n_slots + t]
            s = lax.dot_general(qb, kbuf[slot, h, t].astype(BF16), _TRANS_B, preferred_element_type=F32) * ATTN_SCALE
            s = s + bias_ref[h, pl.ds(jnp.maximum(lp, 0), 1), :]
            s_list.append(jnp.where(lp >= 0, s, NEG_INF))
        k_new = kn_ref[0, h:h + 1, :].astype(BF16).astype(F32)
        v_new = vn_ref[0, h:h + 1, :].astype(BF16).astype(F32)
        s_new = jnp.sum(qb.astype(F32) * k_new, axis=-1, keepdims=True) * ATTN_SCALE + bias0_ref[h]
        mx = s_new
        for s in s_list:
            mx = jnp.maximum(mx, jnp.max(s, axis=-1, keepdims=True))
        e_new = jnp.exp(s_new - mx)
        den = e_new
        e_list = []
        for s in s_list:
            e = jnp.exp(s - mx)
            e_list.append(e)
            den = den + jnp.sum(e, axis=-1, keepdims=True)
        acc = (e_new / den).astype(BF16).astype(F32) * v_new
        for t in range(n_slots):
            acc = acc + jnp.dot((e_list[t] / den).astype(BF16), vbuf[slot, h, t].astype(BF16),
                                preferred_element_type=F32)
        o_ref[0, h:h + 1, :] = acc[0:1]


def _moba_sample(qkv_s, cache_k, cache_v, layer, page_table, bias_b):
    db = qkv_s.shape[0]
    page = cache_k.shape[2]
    hw = H_B * HEAD_DIM
    n_pages = page_table.shape[1]
    past_len = n_pages * page
    assert past_len % MOBA_BLOCK == 0 and MOBA_BLOCK % page == 0
    own = past_len // MOBA_BLOCK
    per = MOBA_BLOCK // page
    n_slots = MOBA_TOPK * per
    q = qkv_s[:, :hw].reshape(db, H_B, HEAD_DIM)
    k_new = qkv_s[:, hw:2 * hw].reshape(db, H_B, HEAD_DIM)
    v_new = qkv_s[:, 2 * hw:].reshape(db, H_B, HEAD_DIM)

    ksum = _moba_block_sums(cache_k, layer, page_table)
    blocks = _moba_sample_select(q, ksum, own)[:, :, :MOBA_TOPK]
    lpage = jnp.where(blocks[..., None] >= 0, blocks[..., None] * per + jnp.arange(per), -1)
    lpage = lpage.reshape(db, H_B, n_slots)
    phys = jnp.take_along_axis(page_table[:, None, :], jnp.maximum(lpage, 0).reshape(db, 1, -1), axis=2)
    key_pos = jnp.arange(past_len).reshape(n_pages, page)
    bias_tbl = _bias_lookup(bias_b, past_len - key_pos).transpose(2, 0, 1)
    bias0 = bias_b[0].astype(F32).reshape(H_B, 1, 1)

    tok = pl.BlockSpec((1, H_B, HEAD_DIM), lambda b, pg, lp: (b, 0, 0))
    out = pl.pallas_call(
        functools.partial(_moba_sample_attn_kernel, layer=layer, n_slots=n_slots),
        grid_spec=pltpu.PrefetchScalarGridSpec(
            num_scalar_prefetch=2,
            grid=(db,),
            in_specs=[tok, tok, tok,
                      pl.BlockSpec((H_B, n_pages, page), lambda b, pg, lp: (0, 0, 0)),
                      pl.BlockSpec((H_B, 1, 1), lambda b, pg, lp: (0, 0, 0)),
                      pl.BlockSpec(memory_space=pl.ANY),
                      pl.BlockSpec(memory_space=pl.ANY)],
            out_specs=tok,
            scratch_shapes=[pltpu.VMEM((2, H_B, n_slots, page, HEAD_DIM), F32),
                            pltpu.VMEM((2, H_B, n_slots, page, HEAD_DIM), F32),
                            pltpu.SemaphoreType.DMA((2,))]),
        out_shape=jax.ShapeDtypeStruct((db, H_B, HEAD_DIM), F32),
        compiler_params=_params("arbitrary"),
        name="moba_sample_attn",
    )(phys.reshape(-1), lpage.reshape(-1), q, k_new, v_new, bias_tbl, bias0, cache_k, cache_v)
    return out.reshape(db, hw)


def kernel(x_prompt, x_sample, c_prompt, c_sample, cache_a_w128, cache_a_w512, cache_a_w2048, cache_b_k, cache_b_v, page_table, rel_bias, norm_g, w_ada, b_ada, w_ffn_in, w_ffn_out, w_qkv_a, qk_gain_a, w_o_a, w_qkv_b, qk_gain_b, w_o_b):
    batch, seq, d = x_prompt.shape
    db, dec_seq, _ = x_sample.shape
    assert batch == 1 and dec_seq == 1, "one prompt sequence, one new token per decode sequence"
    depth = norm_g.shape[0]
    n_ab = N_GROUPS_A * H_A
    bias_a = rel_bias[:, :n_ab].reshape(N_BUCKETS, N_GROUPS_A, H_A)
    bias_b = rel_bias[:, n_ab:]
    a_bufs = (cache_a_w128, cache_a_w512, cache_a_w2048)
    hw_b = H_B * HEAD_DIM

    rows = batch + db
    c_all = jnp.pad(jnp.concatenate([c_prompt, c_sample], axis=0), ((0, -rows % 8), (0, 0)))
    mod_all = _ada_all(c_all, w_ada, b_ada)

    xp = x_prompt.reshape(seq, d)
    xs = x_sample.reshape(db, d)
    a_new_p = [[] for _ in A_GROUPS]
    a_new_s = [[] for _ in A_GROUPS]
    bk_p, bk_s, bv_p, bv_s = [], [], [], []
    for layer in range(depth):
        mp = mod_all[layer, :batch]
        ms = mod_all[layer, batch:rows]
        hp = _ffn_in(xp, norm_g[layer, 0], mp, 0, w_ffn_in[layer, 0])
        xp = _mm_res(hp, w_ffn_out[layer, 0], xp, mp, 0, FFN_RES)
        hs = _ffn_in(xs, norm_g[layer, 0], ms, 0, w_ffn_in[layer, 0])
        xs = _mm_res(hs, w_ffn_out[layer, 0], xs, ms, 0, FFN_RES)
        i = layer // 2
        if layer % 2 == 0:
            qkv_p, = _qkv(xp, norm_g[layer, 1], mp, w_qkv_a[i], qk_gain_a[i])
            qkv_s, = _qkv(xs, norm_g[layer, 1], ms, w_qkv_a[i], qk_gain_a[i])
            yp = _dilated_prompt(qkv_p, bias_a)
            ys = _dilated_sample(qkv_s, a_bufs, i, bias_a)
            w_o = w_o_a[i]
            kv_p = qkv_p.reshape(seq, 3, N_GROUPS_A, H_A, HEAD_DIM)
            kv_s = qkv_s.reshape(db, 3, N_GROUPS_A, H_A, HEAD_DIM)
            for g, (win, dil) in enumerate(A_GROUPS):
                keep = min(win, seq)
                a_new_p[g].append(jnp.stack([kv_p[seq - keep:, 1, g], kv_p[seq - keep:, 2, g]], axis=2)[None])
                a_new_s[g].append(jnp.stack([kv_s[:, 1, g], kv_s[:, 2, g]], axis=2)[:, None])
        else:
            qkv_p, qkv_p16 = _qkv(xp, norm_g[layer, 1], mp, w_qkv_b[i], qk_gain_b[i], with_bf16=True)
            qkv_s, = _qkv(xs, norm_g[layer, 1], ms, w_qkv_b[i], qk_gain_b[i])
            yp = _moba_prompt(qkv_p, qkv_p16, bias_b)
            ys = _moba_sample(qkv_s, cache_b_k, cache_b_v, i, page_table, bias_b)
            w_o = w_o_b[i]
            bk_p.append(qkv_p[:, hw_b:2 * hw_b].reshape(batch, seq, H_B, HEAD_DIM))
            bv_p.append(qkv_p[:, 2 * hw_b:].reshape(batch, seq, H_B, HEAD_DIM))
            bk_s.append(qkv_s[:, hw_b:2 * hw_b].reshape(db, dec_seq, H_B, HEAD_DIM))
            bv_s.append(qkv_s[:, 2 * hw_b:].reshape(db, dec_seq, H_B, HEAD_DIM))
        xp = _mm_res(yp, w_o, xp, mp, 1, 1.0)
        xs = _mm_res(ys, w_o, xs, ms, 1, 1.0)
        hp = _ffn_in(xp, norm_g[layer, 2], mp, 2, w_ffn_in[layer, 1])
        xp = _mm_res(hp, w_ffn_out[layer, 1], xp, mp, 2, FFN_RES)
        hs = _ffn_in(xs, norm_g[layer, 2], ms, 2, w_ffn_in[layer, 1])
        xs = _mm_res(hs, w_ffn_out[layer, 1], xs, ms, 2, FFN_RES)
    return (xp.reshape(batch, seq, d), xs.reshape(db, dec_seq, d),
            jnp.stack(a_new_p[0]), jnp.stack(a_new_s[0]), jnp.stack(a_new_p[1]), jnp.stack(a_new_s[1]),
            jnp.stack(a_new_p[2]), jnp.stack(a_new_s[2]),
            jnp.stack(bk_p), jnp.stack(bk_s), jnp.stack(bv_p), jnp.stack(bv_s))
```

```python
import functools
import math

import jax
import jax.numpy as jnp
from jax import lax
from jax.experimental import pallas as pl
from jax.experimental.pallas import tpu as pltpu

F32 = jnp.float32
BF16 = jnp.bfloat16

D_MODEL = 2048
HEAD_DIM = 128
A_GROUPS = ((128, 1), (512, 4), (2048, 16))
N_GROUPS_A = 3
H_A = 8
A_Q_BLOCK = 128
H_B = 16
MOBA_BLOCK = 256
MOBA_TOPK = 3
N_BUCKETS = 32
REL_MAX_DIST = 4096
D_FF = 5632
FFN_RES = 0.5
RMS_EPS = 1e-6
NEG_INF = -1e30
ATTN_SCALE = HEAD_DIM ** -0.5
LOG2E = math.log2(math.e)

VMEM_LIMIT_BYTES = 56 * 1024 * 1024
ROW_TILE = 1024
A_SUPER = max(dil for _, dil in A_GROUPS) * A_Q_BLOCK
MOBA_STREAMS = 4

_TRANS_B = (((1,), (1,)), ((), ()))


def _params(*sem):
    return pltpu.CompilerParams(dimension_semantics=sem, vmem_limit_bytes=VMEM_LIMIT_BYTES)


def _t5_bucket(dist):
    n = jnp.maximum(dist, 0)
    max_exact = N_BUCKETS // 2
    nf = jnp.maximum(n, 1).astype(F32)
    large = max_exact + (jnp.log(nf / max_exact) / math.log(REL_MAX_DIST / max_exact)
                         * (N_BUCKETS - max_exact)).astype(jnp.int32)
    large = jnp.minimum(large, N_BUCKETS - 1)
    return jnp.where(n < max_exact, n, large)


def _bias_lookup(table, dist):
    onehot = (_t5_bucket(dist)[..., None] == jnp.arange(N_BUCKETS)).astype(F32)
    return jnp.einsum('...b,bh->...h', onehot, table.astype(F32), precision=lax.Precision.HIGHEST)


def _silu(x):
    return x * jax.nn.sigmoid(x)


def _norm_mod(x, g, scale, shift):
    y = x * lax.rsqrt(jnp.mean(x * x, axis=-1, keepdims=True) + RMS_EPS)
    return (y * g) * (1.0 + scale) + shift


def _ada_kernel(c_ref, w_ref, b_ref, o_ref):
    a = _silu(c_ref[...]).astype(BF16)
    o_ref[...] = jnp.dot(a, w_ref[...].astype(BF16), preferred_element_type=F32) + b_ref[...]


def _ada_all(c_all, w_ada, b_ada):
    depth, d, n = w_ada.shape
    r = c_all.shape[0]
    tn = 1024
    return pl.pallas_call(
        _ada_kernel,
        grid=(depth, n // tn),
        in_specs=[pl.BlockSpec((r, d), lambda l, j: (0, 0)),
                  pl.BlockSpec((None, d, tn), lambda l, j: (l, 0, j)),
                  pl.BlockSpec((None, 1, tn), lambda l, j: (l, 0, j))],
        out_specs=pl.BlockSpec((None, r, tn), lambda l, j: (l, 0, j)),
        out_shape=jax.ShapeDtypeStruct((depth, r, n), F32),
        compiler_params=_params("arbitrary", "arbitrary"),
        name="ada_mod",
    )(c_all, w_ada, b_ada.reshape(depth, 1, n))


def _ffn_in_kernel(x_ref, g_ref, sc_ref, sh_ref, wg_ref, wu_ref, o_ref, h_ref):
    @pl.when(pl.program_id(1) == 0)
    def _():
        h_ref[...] = _norm_mod(x_ref[...], g_ref[...], sc_ref[...], sh_ref[...]).astype(BF16)

    h = h_ref[...]
    a = jnp.dot(h, wg_ref[...].astype(BF16), preferred_element_type=F32)
    u = jnp.dot(h, wu_ref[...].astype(BF16), preferred_element_type=F32)
    o_ref[...] = (_silu(a) * u).astype(BF16)


def _mod_spec(mod, tm, col_block, width):
    if mod.shape[0] == 1:
        return pl.BlockSpec((1, width), lambda i, j: (0, col_block(j)))
    return pl.BlockSpec((tm, width), lambda i, j: (i, col_block(j)))


def _ffn_in(x, g, mod, sub, w_in):
    m, d = x.shape
    f = w_in.shape[1] // 2
    tm = min(m, ROW_TILE)
    tn = 256
    nj = f // tn
    return pl.pallas_call(
        _ffn_in_kernel,
        grid=(m // tm, nj),
        in_specs=[pl.BlockSpec((tm, d), lambda i, j: (i, 0)),
                  pl.BlockSpec((1, d), lambda i, j: (0, 0)),
                  _mod_spec(mod, tm, lambda j: 3 * sub + 1, d),
                  _mod_spec(mod, tm, lambda j: 3 * sub, d),
                  pl.BlockSpec((d, tn), lambda i, j: (0, j)),
                  pl.BlockSpec((d, tn), lambda i, j: (0, j + nj))],
        out_specs=pl.BlockSpec((tm, tn), lambda i, j: (i, j)),
        out_shape=jax.ShapeDtypeStruct((m, f), BF16),
        scratch_shapes=[pltpu.VMEM((tm, d), BF16)],
        compiler_params=_params("arbitrary", "arbitrary"),
        name="ffn_in",
    )(x, g.reshape(1, d), mod, mod, w_in, w_in)


def _mm_res_kernel(a_ref, w_ref, x_ref, gate_ref, o_ref, *, coef):
    acc = jnp.dot(a_ref[...].astype(BF16), w_ref[...].astype(BF16), preferred_element_type=F32)
    o_ref[...] = x_ref[...] + (coef * gate_ref[...]) * acc


def _mm_res(a, w, x, mod, sub, coef):
    m, k = a.shape
    n = w.shape[1]
    tm = min(m, ROW_TILE)
    tn = 256
    per = n // tn
    return pl.pallas_call(
        functools.partial(_mm_res_kernel, coef=coef),
        grid=(m // tm, n // tn),
        in_specs=[pl.BlockSpec((tm, k), lambda i, j: (i, 0)),
                  pl.BlockSpec((k, tn), lambda i, j: (0, j)),
                  pl.BlockSpec((tm, tn), lambda i, j: (i, j)),
                  _mod_spec(mod, tm, lambda j: (3 * sub + 2) * per + j, tn)],
        out_specs=pl.BlockSpec((tm, tn), lambda i, j: (i, j)),
        out_shape=jax.ShapeDtypeStruct((m, n), F32),
        compiler_params=_params("arbitrary", "arbitrary"),
        name="mm_res",
    )(a, w, x, mod)


def _qkv_kernel(x_ref, g_ref, sc_ref, sh_ref, w_ref, gain_ref, o_ref, *rest, tiles_per_part):
    h_ref = rest[-1]
    j = pl.program_id(1)

    @pl.when(j == 0)
    def _():
        h_ref[...] = _norm_mod(x_ref[...], g_ref[...], sc_ref[...], sh_ref[...]).astype(BF16)

    acc = jnp.dot(h_ref[...], w_ref[...].astype(BF16), preferred_element_type=F32)
    part = j // tiles_per_part

    @pl.when(part < 2)
    def _():
        gain = gain_ref[pl.ds(part, 1), :]
        for c in range(acc.shape[1] // HEAD_DIM):
            sl = slice(c * HEAD_DIM, (c + 1) * HEAD_DIM)
            a = acc[:, sl]
            y = (a * lax.rsqrt(jnp.mean(a * a, axis=-1, keepdims=True) + RMS_EPS)) * gain
            o_ref[:, sl] = y
            if len(rest) == 2:
                rest[0][:, sl] = y.astype(BF16)

    @pl.when(part == 2)
    def _():
        o_ref[...] = acc
        if len(rest) == 2:
            rest[0][...] = acc.astype(BF16)


def _qkv(x, g, mod, w_qkv, qk_gain, with_bf16=False):
    m, d = x.shape
    n = w_qkv.shape[1]
    tm = min(m, ROW_TILE)
    tn = 512
    ospec = pl.BlockSpec((tm, tn), lambda i, j: (i, j))
    out_shape = [jax.ShapeDtypeStruct((m, n), F32)]
    if with_bf16:
        out_shape.append(jax.ShapeDtypeStruct((m, n), BF16))
    return pl.pallas_call(
        functools.partial(_qkv_kernel, tiles_per_part=n // 3 // tn),
        grid=(m // tm, n // tn),
        in_specs=[pl.BlockSpec((tm, d), lambda i, j: (i, 0)),
                  pl.BlockSpec((1, d), lambda i, j: (0, 0)),
                  _mod_spec(mod, tm, lambda j: 4, d),
                  _mod_spec(mod, tm, lambda j: 3, d),
                  pl.BlockSpec((d, tn), lambda i, j: (0, j)),
                  pl.BlockSpec((2, HEAD_DIM), lambda i, j: (0, 0))],
        out_specs=[ospec] * len(out_shape),
        out_shape=out_shape,
        scratch_shapes=[pltpu.VMEM((tm, d), BF16)],
        compiler_params=_params("arbitrary", "arbitrary"),
        name="qkv",
    )(x, g.reshape(1, d), mod, mod, w_qkv, qk_gain)


def _dil_kernel(*refs):
    n_in = 5 * N_GROUPS_A
    bias_ref, o_ref, og_ref, lg_ref = refs[n_in:]
    qb = A_Q_BLOCK
    sb = pl.program_id(0)
    col = lax.broadcasted_iota(jnp.int32, (qb, 2 * qb), 1)
    prev_exists = (sb > 0) | (col >= qb)
    for g, (_, dil) in enumerate(A_GROUPS):
        q_ref, kc_ref, kp_ref, vc_ref, vp_ref = refs[5 * g:5 * g + 5]
        nblk = A_SUPER // (dil * qb)
        bias = bias_ref[g]

        def rows_of(n, r):
            return pl.ds(n * qb * dil + r, qb, stride=dil) if dil > 1 else pl.ds(n * qb, qb)

        for r in range(dil):
            for n in range(nblk):
                rows = rows_of(n, r)
                if n > 0:
                    k_prev, v_prev = kc_ref[rows_of(n - 1, r), :], vc_ref[rows_of(n - 1, r), :]
                else:
                    k_prev, v_prev = kp_ref[rows_of(nblk - 1, r), :], vp_ref[rows_of(nblk - 1, r), :]
                q = q_ref[rows, :].astype(BF16)
                kk = jnp.concatenate([k_prev, kc_ref[rows, :]], axis=0).astype(BF16)
                vv = jnp.concatenate([v_prev, vc_ref[rows, :]], axis=0).astype(BF16)
                s = lax.dot_general(q, kk, _TRANS_B, preferred_element_type=F32) * ATTN_SCALE + bias
                if n == 0:
                    s = jnp.where(prev_exists, s, NEG_INF)
                mx = jnp.max(s, axis=-1, keepdims=True)
                e = jnp.exp(s - mx)
                den = jnp.sum(e, axis=-1, keepdims=True)
                og_ref[g, rows, :] = jnp.dot((e / den).astype(BF16), vv, preferred_element_type=F32)
                lg_ref[g, rows, :] = mx + jnp.log(den)
    l0, l1, l2 = lg_ref[0], lg_ref[1], lg_ref[2]
    mx = jnp.maximum(jnp.maximum(l0, l1), l2)
    e0, e1, e2 = jnp.exp(l0 - mx), jnp.exp(l1 - mx), jnp.exp(l2 - mx)
    den = e0 + e1 + e2
    o_ref[...] = ((e0 / den) * og_ref[0] + (e1 / den) * og_ref[1] + (e2 / den) * og_ref[2]).astype(o_ref.dtype)


def _dilated_prompt(qkv, bias_a):
    s, n = qkv.shape
    qb = A_Q_BLOCK
    assert s % A_SUPER == 0
    gh = N_GROUPS_A * H_A
    qi = qb + jnp.arange(qb)
    kj = jnp.arange(2 * qb)
    diff = qi[:, None] - kj[None, :]
    tables = []
    for g, (win, dil) in enumerate(A_GROUPS):
        band = (diff >= 0) & (diff <= win // dil)
        bias = _bias_lookup(bias_a[:, g], diff * dil).transpose(2, 0, 1)
        tables.append(jnp.where(band[None], bias, NEG_INF))
    bias = jnp.stack(tables)

    in_specs, operands = [], []
    for g in range(N_GROUPS_A):
        for part, prev in ((0, False), (1, False), (1, True), (2, False), (2, True)):
            cb = part * gh + g * H_A
            if prev:
                in_specs.append(pl.BlockSpec((A_SUPER, HEAD_DIM),
                                             lambda sb, h, cb=cb: (jnp.maximum(sb - 1, 0), cb + h)))
            else:
                in_specs.append(pl.BlockSpec((A_SUPER, HEAD_DIM), lambda sb, h, cb=cb: (sb, cb + h)))
            operands.append(qkv)
    in_specs.append(pl.BlockSpec((N_GROUPS_A, None, qb, 2 * qb), lambda sb, h: (0, h, 0, 0)))
    return pl.pallas_call(
        _dil_kernel,
        grid=(s // A_SUPER, H_A),
        in_specs=in_specs,
        out_specs=pl.BlockSpec((A_SUPER, HEAD_DIM), lambda sb, h: (sb, h)),
        out_shape=jax.ShapeDtypeStruct((s, H_A * HEAD_DIM), BF16),
        scratch_shapes=[pltpu.VMEM((N_GROUPS_A, A_SUPER, HEAD_DIM), F32),
                        pltpu.VMEM((N_GROUPS_A, A_SUPER, 1), F32)],
        compiler_params=_params("arbitrary", "arbitrary"),
        name="dilated_prompt",
    )(*operands, bias)


def _dil_sample_kernel(q_ref, b0_ref, b1_ref, b2_ref, bias_ref, o_ref):
    hw = H_A * HEAD_DIM
    row = q_ref[0]
    outs, lses = [], []
    for g, buf_ref in enumerate((b0_ref, b1_ref, b2_ref)):
        o_g, l_g = [], []
        for h in range(H_A):
            c = g * hw + h * HEAD_DIM
            q = row[:, c:c + HEAD_DIM]
            k_new = row[:, 3 * hw + c:3 * hw + c + HEAD_DIM]
            v_new = row[:, 6 * hw + c:6 * hw + c + HEAD_DIM]
            kb = buf_ref[:, h, 0, :]
            vb = buf_ref[:, h, 1, :]
            bias = bias_ref[g * H_A + h]
            nb = kb.shape[0]
            s_buf = jnp.sum(kb * q, axis=-1, keepdims=True) * ATTN_SCALE + bias[:nb]
            s_new = jnp.sum(k_new * q, axis=-1, keepdims=True) * ATTN_SCALE + bias[nb:nb + 1]
            mx = jnp.maximum(jnp.max(s_buf, axis=0, keepdims=True), s_new)
            e_buf = jnp.exp(s_buf - mx)
            e_new = jnp.exp(s_new - mx)
            den = jnp.sum(e_buf, axis=0, keepdims=True) + e_new
            o = (jnp.sum((e_buf / den) * vb, axis=0, keepdims=True) + (e_new / den) * v_new)
            o_g.append(o)
            l_g.append(mx + jnp.log(den))
        outs.append(o_g)
        lses.append(l_g)
    for h in range(H_A):
        l0, l1, l2 = lses[0][h], lses[1][h], lses[2][h]
        mx = jnp.maximum(jnp.maximum(l0, l1), l2)
        e0, e1, e2 = jnp.exp(l0 - mx), jnp.exp(l1 - mx), jnp.exp(l2 - mx)
        den = e0 + e1 + e2
        o_ref[0, :, h * HEAD_DIM:(h + 1) * HEAD_DIM] = (
            (e0 / den) * outs[0][h] + (e1 / den) * outs[1][h] + (e2 / den) * outs[2][h])


def _dilated_sample(qkv_s, bufs, layer, bias_a):
    db, n = qkv_s.shape
    hw = H_A * HEAD_DIM
    views, biases, specs = [], [], []
    for g, (win, dil) in enumerate(A_GROUPS):
        n_back = win // dil
        n_layers, _, lb = bufs[g].shape[:3]
        assert lb == win and lb % dil == 0, "window buffer must hold the full window"
        views.append(bufs[g].reshape(n_layers * db, n_back, dil, H_A, 2, HEAD_DIM))
        specs.append(pl.BlockSpec((None, n_back, None, H_A, 2, HEAD_DIM),
                                  lambda b: (layer * db + b, 0, 0, 0, 0, 0)))
        j = jnp.concatenate([n_back - jnp.arange(n_back), jnp.zeros((8,), jnp.int32)])
        biases.append(_bias_lookup(bias_a[:, g], j * dil).T)
    bias = jnp.concatenate(biases, axis=0)[:, :, None]
    out = pl.pallas_call(
        _dil_sample_kernel,
        grid=(db,),
        in_specs=[pl.BlockSpec((1, 1, n), lambda b: (b, 0, 0))] + specs
                 + [pl.BlockSpec(bias.shape, lambda b: (0, 0, 0))],
        out_specs=pl.BlockSpec((1, 1, hw), lambda b: (b, 0, 0)),
        out_shape=jax.ShapeDtypeStruct((db, 1, hw), F32),
        compiler_params=_params("arbitrary"),
        name="dilated_sample",
    )(qkv_s.reshape(db, 1, n), *views, bias)
    return out.reshape(db, hw)


def _heads_kernel(k_ref, v_ref, k3_ref, v3_ref):
    for c in range(H_B):
        sl = slice(c * HEAD_DIM, (c + 1) * HEAD_DIM)
        k3_ref[:, c, :] = k_ref[:, sl]
        v3_ref[:, c, :] = v_ref[:, sl]


def _kv_heads(qkv):
    s = qkv.shape[0]
    hw = H_B * HEAD_DIM
    tm = 512
    ospec = pl.BlockSpec((tm, H_B, HEAD_DIM), lambda i: (i, 0, 0))
    oshape = jax.ShapeDtypeStruct((s, H_B, HEAD_DIM), qkv.dtype)
    return pl.pallas_call(
        _heads_kernel,
        grid=(s // tm,),
        in_specs=[pl.BlockSpec((tm, hw), lambda i: (i, 1)), pl.BlockSpec((tm, hw), lambda i: (i, 2))],
        out_specs=[ospec, ospec],
        out_shape=[oshape, oshape],
        compiler_params=_params("arbitrary"),
        name="kv_heads",
    )(qkv, qkv)


def _kmean_kernel(k_ref, o_ref):
    for b in range(o_ref.shape[0]):
        o_ref[b:b + 1, :] = jnp.sum(k_ref[b * MOBA_BLOCK:(b + 1) * MOBA_BLOCK, :], axis=0,
                                    keepdims=True) / MOBA_BLOCK


def _moba_kmean(qkv):
    s = qkv.shape[0]
    hw = H_B * HEAD_DIM
    per = 8
    nb = s // MOBA_BLOCK
    return pl.pallas_call(
        _kmean_kernel,
        grid=(nb // per,),
        in_specs=[pl.BlockSpec((per * MOBA_BLOCK, hw), lambda i: (i, 1))],
        out_specs=pl.BlockSpec((per, hw), lambda i: (i, 0)),
        out_shape=jax.ShapeDtypeStruct((nb, hw), F32),
        compiler_params=_params("arbitrary"),
        name="moba_kmean",
    )(qkv)


def _top3_mask(sc, own):
    lane = lax.broadcasted_iota(jnp.int32, sc.shape, 1)
    lane_f = lane.astype(F32)
    past = lane < own
    cur = jnp.where(past, sc, NEG_INF)
    sel = lane == own
    for _ in range(MOBA_TOPK):
        mx = jnp.max(cur, axis=-1, keepdims=True)
        idx = jnp.min(jnp.where(cur == mx, lane_f, float(sc.shape[1])), axis=-1, keepdims=True)
        pick = lane_f == idx
        sel = sel | (pick & past)
        cur = jnp.where(pick, -jnp.inf, cur)
    return jnp.where(sel, 0.0, NEG_INF)


def _moba_select_kernel(q_ref, km_ref, o_ref, *, tq):
    q = q_ref[...]
    sc = lax.dot_general(q, km_ref[...], _TRANS_B, preferred_element_type=F32,
                         precision=lax.Precision.HIGHEST)
    pos = pl.program_id(1) * tq + lax.broadcasted_iota(jnp.int32, (tq, 1), 0)
    o_ref[:, :HEAD_DIM] = q.astype(BF16)
    o_ref[:, HEAD_DIM:] = _top3_mask(sc, pos // MOBA_BLOCK).astype(BF16)


def _moba_select(qkv, kmean):
    s = qkv.shape[0]
    nb = kmean.shape[0]
    assert nb <= HEAD_DIM
    tq = 1024
    km = jnp.pad(kmean, ((0, HEAD_DIM - nb), (0, 0)))
    return pl.pallas_call(
        functools.partial(_moba_select_kernel, tq=tq),
        grid=(H_B, s // tq),
        in_specs=[pl.BlockSpec((tq, HEAD_DIM), lambda h, t: (t, h)),
                  pl.BlockSpec((HEAD_DIM, HEAD_DIM), lambda h, t: (0, h))],
        out_specs=pl.BlockSpec((None, tq, 2 * HEAD_DIM), lambda h, t: (h, t, 0)),
        out_shape=jax.ShapeDtypeStruct((H_B, s, 2 * HEAD_DIM), BF16),
        compiler_params=_params("arbitrary", "arbitrary"),
        name="moba_select",
    )(qkv, km)


def _moba_attn_kernel(qx_ref, k_ref, v_ref, rev_ref, o_ref, kx_ref, vx_ref, t_ref, acc_ref, m_ref, l_ref, *, nb):
    blk = MOBA_BLOCK
    c1 = ATTN_SCALE * LOG2E
    lane = lax.broadcasted_iota(jnp.int32, (blk, HEAD_DIM), 1)
    row = lax.broadcasted_iota(jnp.int32, (blk, blk), 0)
    col = lax.broadcasted_iota(jnp.int32, (blk, blk), 1)

    def block_rows(b):
        return pl.ds(pl.multiple_of(b * blk, blk), blk)

    def fill(b, carry):
        rows = block_rows(b)
        kx_ref[rows, :HEAD_DIM] = k_ref[rows, :]
        kx_ref[rows, HEAD_DIM:] = jnp.where(lane == b, 1.0, 0.0).astype(BF16)
        vx_ref[rows, :HEAD_DIM] = v_ref[rows, :]
        vx_ref[rows, HEAD_DIM:] = jnp.ones((blk, HEAD_DIM), BF16)
        return carry
    lax.fori_loop(0, nb, fill, 0)

    def set_bias_tile(d):
        vec = jnp.concatenate([rev_ref[pl.ds(nb - d, 1), :], rev_ref[pl.ds(nb + 1 - d, 1), :]], axis=1)
        t = pltpu.roll(jnp.broadcast_to(vec, (blk, 2 * blk)), 0, 1, stride=1, stride_axis=0)[:, blk:]
        t_ref[...] = jnp.where(col <= row, t, NEG_INF) if isinstance(d, int) and d == 0 else t

    def scores(i, d):
        qrows, krows = block_rows(i), block_rows(i - d)
        s = lax.dot_general(qx_ref[qrows, :], kx_ref[krows, :], _TRANS_B, preferred_element_type=F32)
        return qrows, krows, s * c1 + t_ref[...]

    def first_tiles(ids, d):
        for i in ids:
            qrows, krows, s = scores(i, d)
            m = jnp.max(s, axis=-1, keepdims=True)
            pv = jnp.dot(jnp.exp2(s - m).astype(BF16), vx_ref[krows, :], preferred_element_type=F32)
            m_ref[qrows, :] = jnp.broadcast_to(m, (blk, HEAD_DIM))
            l_ref[qrows, :] = pv[:, HEAD_DIM:]
            acc_ref[qrows, :] = pv[:, :HEAD_DIM]

    def later_tiles(ids, d):
        loaded = []
        for i in ids:
            qrows, krows, s = scores(i, d)
            loaded.append((qrows, krows, s, m_ref[qrows, :], l_ref[qrows, :], acc_ref[qrows, :]))
        results = []
        for qrows, krows, s, m_old, l_old, acc_old in loaded:
            m_new = jnp.maximum(m_old, jnp.max(s, axis=-1, keepdims=True))
            alpha = jnp.exp2(m_old - m_new)
            p = jnp.exp2(s - jnp.concatenate([m_new] * (blk // HEAD_DIM), axis=1))
            pv = jnp.dot(p.astype(BF16), vx_ref[krows, :], preferred_element_type=F32)
            results.append((qrows, m_new, alpha * l_old + pv[:, HEAD_DIM:], alpha * acc_old + pv[:, :HEAD_DIM]))
        for qrows, m_new, l_new, acc_new in results:
            m_ref[qrows, :] = m_new
            l_ref[qrows, :] = l_new
            acc_ref[qrows, :] = acc_new

    u = MOBA_STREAMS
    assert u == 4

    def run(tiles, d):
        n_main = (nb - d) // u

        def main_body(it, carry):
            tiles([d + it * u + k for k in range(u)], d)
            return carry
        lax.fori_loop(0, n_main, main_body, 0)
        base = d + n_main * u
        rest = (nb - d) - n_main * u
        if isinstance(d, int) and (nb - d) % u == 0:
            return

        @pl.when(rest >= 2)
        def _():
            tiles([base, base + 1], d)

        @pl.when(rest % 2 == 1)
        def _():
            tiles([base + rest - 1], d)

    set_bias_tile(0)
    run(first_tiles, 0)

    def per_offset(d, carry):
        set_bias_tile(d)
        run(later_tiles, d)
        return carry
    lax.fori_loop(1, nb, per_offset, 0)

    def finish(b, carry):
        rows = block_rows(b)
        o_ref[rows, :] = (acc_ref[rows, :] / l_ref[rows, :]).astype(o_ref.dtype)
        return carry
    lax.fori_loop(0, nb, finish, 0)


def _moba_prompt(qkv, qkv_bf16, bias_b):
    s = qkv.shape[0]
    blk = MOBA_BLOCK
    nb = s // blk
    kmean = _moba_kmean(qkv)
    qx = _moba_select(qkv, kmean)
    dist = (nb + 1) * blk - jnp.arange((nb + 2) * blk)
    rev = (_bias_lookup(bias_b, dist) * LOG2E).T.reshape(H_B, nb + 2, blk)
    return pl.pallas_call(
        functools.partial(_moba_attn_kernel, nb=nb),
        grid=(H_B,),
        in_specs=[pl.BlockSpec((None, s, 2 * HEAD_DIM), lambda h: (h, 0, 0)),
                  pl.BlockSpec((s, HEAD_DIM), lambda h: (0, H_B + h)),
                  pl.BlockSpec((s, HEAD_DIM), lambda h: (0, 2 * H_B + h)),
                  pl.BlockSpec((None, nb + 2, blk), lambda h: (h, 0, 0))],
        out_specs=pl.BlockSpec((s, HEAD_DIM), lambda h: (0, h)),
        out_shape=jax.ShapeDtypeStruct((s, H_B * HEAD_DIM), BF16),
        scratch_shapes=[pltpu.VMEM((s, 2 * HEAD_DIM), BF16),
                        pltpu.VMEM((s, 2 * HEAD_DIM), BF16),
                        pltpu.VMEM((blk, blk), F32),
                        pltpu.VMEM((s, HEAD_DIM), F32),
                        pltpu.VMEM((s, HEAD_DIM), F32),
                        pltpu.VMEM((s, HEAD_DIM), F32)],
        compiler_params=_params("arbitrary"),
        name="moba_prompt",
    )(qx, qkv_bf16, qkv_bf16, rev)


def _page_sum_kernel(pt_ref, *refs, per):
    del pt_ref
    page_refs, o_ref = refs[:-1], refs[-1]
    step = pl.program_id(1)
    for t in range(len(page_refs) // per):
        acc = jnp.sum(page_refs[per * t][0], axis=0)
        for u in range(1, per):
            acc = acc + jnp.sum(page_refs[per * t + u][0], axis=0)
        o_ref[0, pl.ds(step * (len(page_refs) // per) + t, 1)] = acc[None]


def _moba_block_sums(cache_k, layer, page_table):
    n_layers, n_phys, page = cache_k.shape[:3]
    db, n_pages = page_table.shape
    per = MOBA_BLOCK // page
    nb = n_pages // per
    pages_per_step = 8
    assert n_pages % pages_per_step == 0 and pages_per_step % per == 0
    ck = cache_k.reshape(n_layers * n_phys, page, H_B, HEAD_DIM)

    def pspec(t):
        return pl.BlockSpec((1, page, H_B, HEAD_DIM),
                            lambda b, n, pt: (layer * n_phys + pt[b * n_pages + pages_per_step * n + t], 0, 0, 0))

    return pl.pallas_call(
        functools.partial(_page_sum_kernel, per=per),
        grid_spec=pltpu.PrefetchScalarGridSpec(
            num_scalar_prefetch=1,
            grid=(db, n_pages // pages_per_step),
            in_specs=[pspec(t) for t in range(pages_per_step)],
            out_specs=pl.BlockSpec((1, nb, H_B, HEAD_DIM), lambda b, n, pt: (b, 0, 0, 0))),
        out_shape=jax.ShapeDtypeStruct((db, nb, H_B, HEAD_DIM), F32),
        compiler_params=_params("arbitrary", "arbitrary"),
        name="moba_page_sums",
    )(page_table.reshape(-1), *([ck] * pages_per_step))


def _moba_sample_select_kernel(q_ref, ks_ref, o_ref, *, own):
    q = q_ref[0]
    nb = ks_ref.shape[1]
    lane = lax.broadcasted_iota(jnp.int32, (H_B, HEAD_DIM), 1)
    sc = jnp.zeros((H_B, HEAD_DIM), F32)
    for n in range(nb):
        kmean = ks_ref[0, n] / MOBA_BLOCK
        sc = jnp.where(lane == n, jnp.sum(q * kmean, axis=-1, keepdims=True), sc)
    mask = _top3_mask(sc, jnp.full((H_B, 1), own, jnp.int32))
    lane_f = lane.astype(F32)
    chosen = (mask == 0.0) & (lane < own)
    out = jnp.full((H_B, HEAD_DIM), -1.0, F32)
    cur = jnp.where(chosen, lane_f, float(HEAD_DIM))
    for r in range(MOBA_TOPK):
        idx = jnp.min(cur, axis=-1, keepdims=True)
        out = jnp.where(lane == r, jnp.where(idx < HEAD_DIM, idx, -1.0), out)
        cur = jnp.where(cur == idx, float(HEAD_DIM), cur)
    o_ref[0] = out.astype(jnp.int32)


def _moba_sample_select(q, ksum, own):
    db, nb = ksum.shape[:2]
    return pl.pallas_call(
        functools.partial(_moba_sample_select_kernel, own=own),
        grid=(db,),
        in_specs=[pl.BlockSpec((1, H_B, HEAD_DIM), lambda b: (b, 0, 0)),
                  pl.BlockSpec((1, nb, H_B, HEAD_DIM), lambda b: (b, 0, 0, 0))],
        out_specs=pl.BlockSpec((1, H_B, HEAD_DIM), lambda b: (b, 0, 0)),
        out_shape=jax.ShapeDtypeStruct((db, H_B, HEAD_DIM), jnp.int32),
        compiler_params=_params("arbitrary"),
        name="moba_sample_select",
    )(q, ksum)


def _moba_sample_attn_kernel(pages_ref, lpage_ref, q_ref, kn_ref, vn_ref, bias_ref, bias0_ref, k_hbm, v_hbm,
                             o_ref, kbuf, vbuf, sem, *, layer, n_slots):
    b = pl.program_id(0)
    slot = b % 2

    def copies(seq, buf):
        out = []
        for h in range(H_B):
            for t in range(n_slots):
                pg = pages_ref[(seq * H_B + h) * n_slots + t]
                out.append(pltpu.make_async_copy(k_hbm.at[layer, pg, :, h, :], kbuf.at[buf, h, t], sem.at[buf]))
                out.append(pltpu.make_async_copy(v_hbm.at[layer, pg, :, h, :], vbuf.at[buf, h, t], sem.at[buf]))
        return out

    @pl.when(b == 0)
    def _():
        for cp in copies(0, 0):
            cp.start()

    @pl.when(b + 1 < pl.num_programs(0))
    def _():
        for cp in copies(b + 1, 1 - slot):
            cp.start()

    for cp in copies(b, slot):
        cp.wait()
    lane = lax.broadcasted_iota(jnp.int32, bias_ref.shape[1:], 1)
    for h in range(H_B):
        q = q_ref[0, h:h + 1, :]
        s_list = []
        for t in range(n_slots):
            lp = lpage_ref[(b * H_B + h) * n_slots + t]
            s = jnp.sum(kbuf[slot, h, t] * q, axis=-1, keepdims=True) * ATTN_SCALE
            bias = jnp.sum(jnp.where(lane == lp, bias_ref[h], 0.0), axis=-1, keepdims=True)
            s_list.append(jnp.where(lp >= 0, s + bias, NEG_INF))
        s_new = jnp.sum(q * kn_ref[0, h:h + 1, :], axis=-1, keepdims=True) * ATTN_SCALE + bias0_ref[h]
        mx = s_new
        for s in s_list:
            mx = jnp.maximum(mx, jnp.max(s, axis=0, keepdims=True))
        e_new = jnp.exp(s_new - mx)
        e_list = [jnp.exp(s - mx) for s in s_list]
        den = e_new
        for e in e_list:
            den = den + jnp.sum(e, axis=0, keepdims=True)
        acc = (e_new / den) * vn_ref[0, h:h + 1, :]
        for t in range(n_slots):
            acc = acc + jnp.sum((e_list[t] / den) * vbuf[slot, h, t], axis=0, keepdims=True)
        o_ref[0, h:h + 1, :] = acc


def _moba_sample(qkv_s, cache_k, cache_v, layer, page_table, bias_b):
    db = qkv_s.shape[0]
    page = cache_k.shape[2]
    hw = H_B * HEAD_DIM
    n_pages = page_table.shape[1]
    past_len = n_pages * page
    assert past_len % MOBA_BLOCK == 0 and MOBA_BLOCK % page == 0
    own = past_len // MOBA_BLOCK
    per = MOBA_BLOCK // page
    n_slots = MOBA_TOPK * per
    q = qkv_s[:, :hw].reshape(db, H_B, HEAD_DIM)
    k_new = qkv_s[:, hw:2 * hw].reshape(db, H_B, HEAD_DIM)
    v_new = qkv_s[:, 2 * hw:].reshape(db, H_B, HEAD_DIM)

    ksum = _moba_block_sums(cache_k, layer, page_table)
    blocks = _moba_sample_select(q, ksum, own)[:, :, :MOBA_TOPK]
    lpage = jnp.where(blocks[..., None] >= 0, blocks[..., None] * per + jnp.arange(per), -1)
    lpage = lpage.reshape(db, H_B, n_slots)
    phys = jnp.take_along_axis(page_table[:, None, :], jnp.maximum(lpage, 0).reshape(db, 1, -1), axis=2)
    key_pos = jnp.arange(past_len).reshape(n_pages, page)
    bias_tbl = _bias_lookup(bias_b, past_len - key_pos).transpose(2, 1, 0)
    bias0 = bias_b[0].astype(F32).reshape(H_B, 1, 1)

    tok = pl.BlockSpec((1, H_B, HEAD_DIM), lambda b, pg, lp: (b, 0, 0))
    out = pl.pallas_call(
        functools.partial(_moba_sample_attn_kernel, layer=layer, n_slots=n_slots),
        grid_spec=pltpu.PrefetchScalarGridSpec(
            num_scalar_prefetch=2,
            grid=(db,),
            in_specs=[tok, tok, tok,
                      pl.BlockSpec((H_B, page, n_pages), lambda b, pg, lp: (0, 0, 0)),
                      pl.BlockSpec((H_B, 1, 1), lambda b, pg, lp: (0, 0, 0)),
                      pl.BlockSpec(memory_space=pl.ANY),
                      pl.BlockSpec(memory_space=pl.ANY)],
            out_specs=tok,
            scratch_shapes=[pltpu.VMEM((2, H_B, n_slots, page, HEAD_DIM), F32),
                            pltpu.VMEM((2, H_B, n_slots, page, HEAD_DIM), F32),
                            pltpu.SemaphoreType.DMA((2,))]),
        out_shape=jax.ShapeDtypeStruct((db, H_B, HEAD_DIM), F32),
        compiler_params=_params("arbitrary"),
        name="moba_sample_attn",
    )(phys.reshape(-1), lpage.reshape(-1), q, k_new, v_new, bias_tbl, bias0, cache_k, cache_v)
    return out.reshape(db, hw)


def kernel(x_prompt, x_sample, c_prompt, c_sample, cache_a_w128, cache_a_w512, cache_a_w2048, cache_b_k, cache_b_v, page_table, rel_bias, norm_g, w_ada, b_ada, w_ffn_in, w_ffn_out, w_qkv_a, qk_gain_a, w_o_a, w_qkv_b, qk_gain_b, w_o_b):
    batch, seq, d = x_prompt.shape
    db, dec_seq, _ = x_sample.shape
    assert batch == 1 and dec_seq == 1, "one prompt sequence, one new token per decode sequence"
    depth = norm_g.shape[0]
    n_ab = N_GROUPS_A * H_A
    bias_a = rel_bias[:, :n_ab].reshape(N_BUCKETS, N_GROUPS_A, H_A)
    bias_b = rel_bias[:, n_ab:]
    a_bufs = (cache_a_w128, cache_a_w512, cache_a_w2048)
    hw_b = H_B * HEAD_DIM

    rows = batch + db
    c_all = jnp.pad(jnp.concatenate([c_prompt, c_sample], axis=0), ((0, -rows % 8), (0, 0)))
    mod_all = _ada_all(c_all, w_ada, b_ada)

    xp = x_prompt.reshape(seq, d)
    xs = x_sample.reshape(db, d)
    a_new_p = [[] for _ in A_GROUPS]
    a_new_s = [[] for _ in A_GROUPS]
    bk_p, bk_s, bv_p, bv_s = [], [], [], []
    for layer in range(depth):
        mp = mod_all[layer, :batch]
        ms = mod_all[layer, batch:rows]
        hp = _ffn_in(xp, norm_g[layer, 0], mp, 0, w_ffn_in[layer, 0])
        xp = _mm_res(hp, w_ffn_out[layer, 0], xp, mp, 0, FFN_RES)
        hs = _ffn_in(xs, norm_g[layer, 0], ms, 0, w_ffn_in[layer, 0])
        xs = _mm_res(hs, w_ffn_out[layer, 0], xs, ms, 0, FFN_RES)
        i = layer // 2
        if layer % 2 == 0:
            qkv_p, = _qkv(xp, norm_g[layer, 1], mp, w_qkv_a[i], qk_gain_a[i])
            qkv_s, = _qkv(xs, norm_g[layer, 1], ms, w_qkv_a[i], qk_gain_a[i])
            yp = _dilated_prompt(qkv_p, bias_a)
            ys = _dilated_sample(qkv_s, a_bufs, i, bias_a)
            w_o = w_o_a[i]
            kv_p = qkv_p.reshape(seq, 3, N_GROUPS_A, H_A, HEAD_DIM)
            kv_s = qkv_s.reshape(db, 3, N_GROUPS_A, H_A, HEAD_DIM)
            for g, (win, dil) in enumerate(A_GROUPS):
                keep = min(win, seq)
                a_new_p[g].append(jnp.stack([kv_p[seq - keep:, 1, g], kv_p[seq - keep:, 2, g]], axis=2)[None])
                a_new_s[g].append(jnp.stack([kv_s[:, 1, g], kv_s[:, 2, g]], axis=2)[:, None])
        else:
            qkv_p, qkv_p16 = _qkv(xp, norm_g[layer, 1], mp, w_qkv_b[i], qk_gain_b[i], with_bf16=True)
            qkv_s, = _qkv(xs, norm_g[layer, 1], ms, w_qkv_b[i], qk_gain_b[i])
            yp = _moba_prompt(qkv_p, qkv_p16, bias_b)
            ys = _moba_sample(qkv_s, cache_b_k, cache_b_v, i, page_table, bias_b)
            w_o = w_o_b[i]
            k_heads, v_heads = _kv_heads(qkv_p)
            bk_p.append(k_heads[None])
            bv_p.append(v_heads[None])
            bk_s.append(qkv_s[:, hw_b:2 * hw_b].reshape(db, dec_seq, H_B, HEAD_DIM))
            bv_s.append(qkv_s[:, 2 * hw_b:].reshape(db, dec_seq, H_B, HEAD_DIM))
        xp = _mm_res(yp, w_o, xp, mp, 1, 1.0)
        xs = _mm_res(ys, w_o, xs, ms, 1, 1.0)
        hp = _ffn_in(xp, norm_g[layer, 2], mp, 2, w_ffn_in[layer, 1])
        xp = _mm_res(hp, w_ffn_out[layer, 1], xp, mp, 2, FFN_RES)
        hs = _ffn_in(xs, norm_g[layer, 2], ms, 2, w_ffn_in[layer, 1])
        xs = _mm_res(hs, w_ffn_out[layer, 1], xs, ms, 2, FFN_RES)
    return (xp.reshape(batch, seq, d), xs.reshape(db, dec_seq, d),
            jnp.stack(a_new_p[0]), jnp.stack(a_new_s[0]), jnp.stack(a_new_p[1]), jnp.stack(a_new_s[1]),
            jnp.stack(a_new_p[2]), jnp.stack(a_new_s[2]),
            jnp.stack(bk_p), jnp.stack(bk_s), jnp.stack(bv_p), jnp.stack(bv_s))
```

```python
import functools
import math

import jax
import jax.numpy as jnp
from jax import lax
from jax.experimental import pallas as pl
from jax.experimental.pallas import tpu as pltpu

F32 = jnp.float32
BF16 = jnp.bfloat16

D_MODEL = 2048
HEAD_DIM = 128
A_GROUPS = ((128, 1), (512, 4), (2048, 16))
N_GROUPS_A = 3
H_A = 8
A_Q_BLOCK = 128
H_B = 16
MOBA_BLOCK = 256
MOBA_TOPK = 3
N_BUCKETS = 32
REL_MAX_DIST = 4096
D_FF = 5632
FFN_RES = 0.5
RMS_EPS = 1e-6
NEG_INF = -1e30
ATTN_SCALE = HEAD_DIM ** -0.5
LOG2E = math.log2(math.e)

VMEM_LIMIT_BYTES = 56 * 1024 * 1024
ROW_TILE = 1024
A_SUPER = max(dil for _, dil in A_GROUPS) * A_Q_BLOCK
MOBA_STREAMS = 4

_TRANS_B = (((1,), (1,)), ((), ()))


def _params(*sem):
    return pltpu.CompilerParams(dimension_semantics=sem, vmem_limit_bytes=VMEM_LIMIT_BYTES)


def _t5_bucket(dist):
    n = jnp.maximum(dist, 0)
    max_exact = N_BUCKETS // 2
    nf = jnp.maximum(n, 1).astype(F32)
    large = max_exact + (jnp.log(nf / max_exact) / math.log(REL_MAX_DIST / max_exact)
                         * (N_BUCKETS - max_exact)).astype(jnp.int32)
    large = jnp.minimum(large, N_BUCKETS - 1)
    return jnp.where(n < max_exact, n, large)


def _bias_lookup(table, dist):
    onehot = (_t5_bucket(dist)[..., None] == jnp.arange(N_BUCKETS)).astype(F32)
    return jnp.einsum('...b,bh->...h', onehot, table.astype(F32), precision=lax.Precision.HIGHEST)


def _silu(x):
    return x * jax.nn.sigmoid(x)


def _norm_mod(x, g, scale, shift):
    y = x * lax.rsqrt(jnp.mean(x * x, axis=-1, keepdims=True) + RMS_EPS)
    return (y * g) * (1.0 + scale) + shift


def _ada_kernel(c_ref, w_ref, b_ref, o_ref):
    a = _silu(c_ref[...]).astype(BF16)
    o_ref[...] = jnp.dot(a, w_ref[...].astype(BF16), preferred_element_type=F32) + b_ref[...]


def _ada_all(c_all, w_ada, b_ada):
    depth, d, n = w_ada.shape
    r = c_all.shape[0]
    tn = 1024
    return pl.pallas_call(
        _ada_kernel,
        grid=(depth, n // tn),
        in_specs=[pl.BlockSpec((r, d), lambda l, j: (0, 0)),
                  pl.BlockSpec((None, d, tn), lambda l, j: (l, 0, j)),
                  pl.BlockSpec((None, 1, tn), lambda l, j: (l, 0, j))],
        out_specs=pl.BlockSpec((None, r, tn), lambda l, j: (l, 0, j)),
        out_shape=jax.ShapeDtypeStruct((depth, r, n), F32),
        compiler_params=_params("arbitrary", "arbitrary"),
        name="ada_mod",
    )(c_all, w_ada, b_ada.reshape(depth, 1, n))


def _on_last_row_tile():
    return pl.program_id(0) == pl.num_programs(0) - 1


def _ffn_in_kernel(x_ref, g_ref, sc_ref, sh_ref, xs_ref, scs_ref, shs_ref, wg_ref, wu_ref, o_ref, os_ref,
                   h_ref, hs_ref):
    first_col = pl.program_id(1) == 0

    @pl.when(first_col)
    def _():
        h_ref[...] = _norm_mod(x_ref[...], g_ref[...], sc_ref[...], sh_ref[...]).astype(BF16)

    wg = wg_ref[...].astype(BF16)
    wu = wu_ref[...].astype(BF16)

    def swiglu(h):
        a = jnp.dot(h, wg, preferred_element_type=F32)
        u = jnp.dot(h, wu, preferred_element_type=F32)
        return (_silu(a) * u).astype(BF16)

    o_ref[...] = swiglu(h_ref[...])

    @pl.when(_on_last_row_tile())
    def _():
        @pl.when(first_col)
        def _():
            hs_ref[...] = _norm_mod(xs_ref[...], g_ref[...], scs_ref[...], shs_ref[...]).astype(BF16)

        os_ref[...] = swiglu(hs_ref[...])


def _lead_spec(w, lead, block, index):
    assert w.ndim == len(lead) + len(block)
    return pl.BlockSpec((None,) * len(lead) + block, lambda i, j: lead + index(i, j))


def _decode_col(n_row_tiles):
    return lambda i, j: jnp.where(i == n_row_tiles - 1, j, 0)


def _ffn_in(xp, xs, g, mp, ms, sub, w_in, lead):
    m, d = xp.shape
    ns = xs.shape[0]
    f = w_in.shape[-1] // 2
    tm = ROW_TILE
    tn = 512
    nj = f // tn
    sj = _decode_col(m // tm)
    return pl.pallas_call(
        _ffn_in_kernel,
        grid=(m // tm, nj),
        in_specs=[pl.BlockSpec((tm, d), lambda i, j: (i, 0)),
                  pl.BlockSpec((1, d), lambda i, j: (0, 0)),
                  pl.BlockSpec((1, d), lambda i, j: (0, 3 * sub + 1)),
                  pl.BlockSpec((1, d), lambda i, j: (0, 3 * sub)),
                  pl.BlockSpec((ns, d), lambda i, j: (0, 0)),
                  pl.BlockSpec((ns, d), lambda i, j: (0, 3 * sub + 1)),
                  pl.BlockSpec((ns, d), lambda i, j: (0, 3 * sub)),
                  _lead_spec(w_in, lead, (d, tn), lambda i, j: (0, j)),
                  _lead_spec(w_in, lead, (d, tn), lambda i, j: (0, j + nj))],
        out_specs=[pl.BlockSpec((tm, tn), lambda i, j: (i, j)),
                   pl.BlockSpec((ns, tn), lambda i, j: (0, sj(i, j)))],
        out_shape=[jax.ShapeDtypeStruct((m, f), BF16), jax.ShapeDtypeStruct((ns, f), BF16)],
        scratch_shapes=[pltpu.VMEM((tm, d), BF16), pltpu.VMEM((ns, d), BF16)],
        compiler_params=_params("arbitrary", "arbitrary"),
        name="ffn_in",
    )(xp, g.reshape(1, d), mp, mp, xs, ms, ms, w_in, w_in)


def _mm_res_kernel(a_ref, x_ref, gate_ref, as_ref, xs_ref, gates_ref, w_ref, o_ref, os_ref, *, coef):
    w = w_ref[...].astype(BF16)
    acc = jnp.dot(a_ref[...].astype(BF16), w, preferred_element_type=F32)
    o_ref[...] = x_ref[...] + (coef * gate_ref[...]) * acc

    @pl.when(_on_last_row_tile())
    def _():
        acc_s = jnp.dot(as_ref[...].astype(BF16), w, preferred_element_type=F32)
        os_ref[...] = xs_ref[...] + (coef * gates_ref[...]) * acc_s


def _mm_res(ap, as_, w, lead, xp, xs, mp, ms, sub, coef):
    m, k = ap.shape
    ns = as_.shape[0]
    n = w.shape[-1]
    tm = ROW_TILE
    tn = 256
    per = n // tn
    sj = _decode_col(m // tm)
    return pl.pallas_call(
        functools.partial(_mm_res_kernel, coef=coef),
        grid=(m // tm, n // tn),
        in_specs=[pl.BlockSpec((tm, k), lambda i, j: (i, 0)),
                  pl.BlockSpec((tm, tn), lambda i, j: (i, j)),
                  pl.BlockSpec((1, tn), lambda i, j: (0, (3 * sub + 2) * per + j)),
                  pl.BlockSpec((ns, k), lambda i, j: (0, 0)),
                  pl.BlockSpec((ns, tn), lambda i, j: (0, sj(i, j))),
                  pl.BlockSpec((ns, tn), lambda i, j: (0, (3 * sub + 2) * per + sj(i, j))),
                  _lead_spec(w, lead, (k, tn), lambda i, j: (0, j))],
        out_specs=[pl.BlockSpec((tm, tn), lambda i, j: (i, j)),
                   pl.BlockSpec((ns, tn), lambda i, j: (0, sj(i, j)))],
        out_shape=[jax.ShapeDtypeStruct((m, n), F32), jax.ShapeDtypeStruct((ns, n), F32)],
        compiler_params=_params("arbitrary", "arbitrary"),
        name="mm_res",
    )(ap, xp, mp, as_, xs, ms, w)


def _qkv_kernel(x_ref, g_ref, sc_ref, sh_ref, xs_ref, scs_ref, shs_ref, w_ref, gain_ref, *rest,
                tiles_per_part, with_bf16):
    o_ref, o16_ref = rest[0], (rest[1] if with_bf16 else None)
    os_ref, h_ref, hs_ref = rest[-3:]
    j = pl.program_id(1)
    part = j // tiles_per_part

    @pl.when(j == 0)
    def _():
        h_ref[...] = _norm_mod(x_ref[...], g_ref[...], sc_ref[...], sh_ref[...]).astype(BF16)

    w = w_ref[...].astype(BF16)

    def emit(acc, out_ref, out16_ref):
        @pl.when(part < 2)
        def _():
            gain = gain_ref[pl.ds(part, 1), :]
            for c in range(acc.shape[1] // HEAD_DIM):
                sl = slice(c * HEAD_DIM, (c + 1) * HEAD_DIM)
                a = acc[:, sl]
                y = (a * lax.rsqrt(jnp.mean(a * a, axis=-1, keepdims=True) + RMS_EPS)) * gain
                out_ref[:, sl] = y
                if out16_ref is not None:
                    out16_ref[:, sl] = y.astype(BF16)

        @pl.when(part == 2)
        def _():
            out_ref[...] = acc
            if out16_ref is not None:
                out16_ref[...] = acc.astype(BF16)

    emit(jnp.dot(h_ref[...], w, preferred_element_type=F32), o_ref, o16_ref)

    @pl.when(_on_last_row_tile())
    def _():
        @pl.when(j == 0)
        def _():
            hs_ref[...] = _norm_mod(xs_ref[...], g_ref[...], scs_ref[...], shs_ref[...]).astype(BF16)

        emit(jnp.dot(hs_ref[...], w, preferred_element_type=F32), os_ref, None)


def _qkv(xp, xs, g, mp, ms, w_qkv, lead, qk_gain, with_bf16=False):
    m, d = xp.shape
    ns = xs.shape[0]
    n = w_qkv.shape[-1]
    tm = ROW_TILE
    tn = 512
    sj = _decode_col(m // tm)
    ospec = pl.BlockSpec((tm, tn), lambda i, j: (i, j))
    out_shape = [jax.ShapeDtypeStruct((m, n), F32)]
    if with_bf16:
        out_shape.append(jax.ShapeDtypeStruct((m, n), BF16))
    out_specs = [ospec] * len(out_shape) + [pl.BlockSpec((ns, tn), lambda i, j: (0, sj(i, j)))]
    out_shape.append(jax.ShapeDtypeStruct((ns, n), F32))
    return pl.pallas_call(
        functools.partial(_qkv_kernel, tiles_per_part=n // 3 // tn, with_bf16=with_bf16),
        grid=(m // tm, n // tn),
        in_specs=[pl.BlockSpec((tm, d), lambda i, j: (i, 0)),
                  pl.BlockSpec((1, d), lambda i, j: (0, 0)),
                  pl.BlockSpec((1, d), lambda i, j: (0, 4)),
                  pl.BlockSpec((1, d), lambda i, j: (0, 3)),
                  pl.BlockSpec((ns, d), lambda i, j: (0, 0)),
                  pl.BlockSpec((ns, d), lambda i, j: (0, 4)),
                  pl.BlockSpec((ns, d), lambda i, j: (0, 3)),
                  _lead_spec(w_qkv, lead, (d, tn), lambda i, j: (0, j)),
                  _lead_spec(qk_gain, lead, (2, HEAD_DIM), lambda i, j: (0, 0))],
        out_specs=out_specs,
        out_shape=out_shape,
        scratch_shapes=[pltpu.VMEM((tm, d), BF16), pltpu.VMEM((ns, d), BF16)],
        compiler_params=_params("arbitrary", "arbitrary"),
        name="qkv",
    )(xp, g.reshape(1, d), mp, mp, xs, ms, ms, w_qkv, qk_gain)


def _dil_kernel(*refs):
    n_in = 5 * N_GROUPS_A
    bias_ref, o_ref, og_ref, lg_ref = refs[n_in:]
    qb = A_Q_BLOCK
    sb = pl.program_id(0)
    col = lax.broadcasted_iota(jnp.int32, (qb, 2 * qb), 1)
    prev_exists = (sb > 0) | (col >= qb)
    for g, (_, dil) in enumerate(A_GROUPS):
        q_ref, kc_ref, kp_ref, vc_ref, vp_ref = refs[5 * g:5 * g + 5]
        nblk = A_SUPER // (dil * qb)
        bias = bias_ref[g]

        def rows_of(n, r):
            return pl.ds(n * qb * dil + r, qb, stride=dil) if dil > 1 else pl.ds(n * qb, qb)

        for r in range(dil):
            for n in range(nblk):
                rows = rows_of(n, r)
                if n > 0:
                    k_prev, v_prev = kc_ref[rows_of(n - 1, r), :], vc_ref[rows_of(n - 1, r), :]
                else:
                    k_prev, v_prev = kp_ref[rows_of(0, r), :], vp_ref[rows_of(0, r), :]
                q = q_ref[rows, :].astype(BF16)
                kk = jnp.concatenate([k_prev, kc_ref[rows, :]], axis=0).astype(BF16)
                vv = jnp.concatenate([v_prev, vc_ref[rows, :]], axis=0).astype(BF16)
                s = lax.dot_general(q, kk, _TRANS_B, preferred_element_type=F32) * ATTN_SCALE + bias
                if n == 0:
                    s = jnp.where(prev_exists, s, NEG_INF)
                mx = jnp.max(s, axis=-1, keepdims=True)
                e = jnp.exp(s - mx)
                den = jnp.sum(e, axis=-1, keepdims=True)
                og_ref[g, rows, :] = jnp.dot((e / den).astype(BF16), vv, preferred_element_type=F32)
                lg_ref[g, rows, :] = mx + jnp.log(den)
    l0, l1, l2 = lg_ref[0], lg_ref[1], lg_ref[2]
    mx = jnp.maximum(jnp.maximum(l0, l1), l2)
    e0, e1, e2 = jnp.exp(l0 - mx), jnp.exp(l1 - mx), jnp.exp(l2 - mx)
    den = e0 + e1 + e2
    o_ref[...] = ((e0 / den) * og_ref[0] + (e1 / den) * og_ref[1] + (e2 / den) * og_ref[2]).astype(o_ref.dtype)


def _dilated_prompt(qkv, bias_a):
    s, n = qkv.shape
    qb = A_Q_BLOCK
    assert s % A_SUPER == 0
    gh = N_GROUPS_A * H_A
    qi = qb + jnp.arange(qb)
    kj = jnp.arange(2 * qb)
    diff = qi[:, None] - kj[None, :]
    tables = []
    for g, (win, dil) in enumerate(A_GROUPS):
        band = (diff >= 0) & (diff <= win // dil)
        bias = _bias_lookup(bias_a[:, g], diff * dil).transpose(2, 0, 1)
        tables.append(jnp.where(band[None], bias, NEG_INF))
    bias = jnp.stack(tables)

    in_specs, operands = [], []
    for g, (_, dil) in enumerate(A_GROUPS):
        span = dil * qb
        per_super = A_SUPER // span
        for part, prev in ((0, False), (1, False), (1, True), (2, False), (2, True)):
            cb = part * gh + g * H_A
            if prev:
                in_specs.append(pl.BlockSpec(
                    (span, HEAD_DIM),
                    lambda sb, h, cb=cb, per_super=per_super: (jnp.maximum(sb * per_super - 1, 0), cb + h)))
            else:
                in_specs.append(pl.BlockSpec((A_SUPER, HEAD_DIM), lambda sb, h, cb=cb: (sb, cb + h)))
            operands.append(qkv)
    in_specs.append(pl.BlockSpec((N_GROUPS_A, None, qb, 2 * qb), lambda sb, h: (0, h, 0, 0)))
    return pl.pallas_call(
        _dil_kernel,
        grid=(s // A_SUPER, H_A),
        in_specs=in_specs,
        out_specs=pl.BlockSpec((A_SUPER, HEAD_DIM), lambda sb, h: (sb, h)),
        out_shape=jax.ShapeDtypeStruct((s, H_A * HEAD_DIM), BF16),
        scratch_shapes=[pltpu.VMEM((N_GROUPS_A, A_SUPER, HEAD_DIM), F32),
                        pltpu.VMEM((N_GROUPS_A, A_SUPER, 1), F32)],
        compiler_params=_params("arbitrary", "arbitrary"),
        name="dilated_prompt",
    )(*operands, bias)


def _dil_sample_kernel(q_ref, b0_ref, b1_ref, b2_ref, bias_ref, o_ref):
    hw = H_A * HEAD_DIM
    row = q_ref[0]
    outs, lses = [], []
    for g, buf_ref in enumerate((b0_ref, b1_ref, b2_ref)):
        o_g, l_g = [], []
        for h in range(H_A):
            c = g * hw + h * HEAD_DIM
            q = row[:, c:c + HEAD_DIM]
            k_new = row[:, 3 * hw + c:3 * hw + c + HEAD_DIM]
            v_new = row[:, 6 * hw + c:6 * hw + c + HEAD_DIM]
            kb = buf_ref[:, h, 0, :]
            vb = buf_ref[:, h, 1, :]
            bias = bias_ref[g * H_A + h]
            nb = kb.shape[0]
            s_buf = jnp.sum(kb * q, axis=-1, keepdims=True) * ATTN_SCALE + bias[:nb]
            s_new = jnp.sum(k_new * q, axis=-1, keepdims=True) * ATTN_SCALE + bias[nb:nb + 1]
            mx = jnp.maximum(jnp.max(s_buf, axis=0, keepdims=True), s_new)
            e_buf = jnp.exp(s_buf - mx)
            e_new = jnp.exp(s_new - mx)
            den = jnp.sum(e_buf, axis=0, keepdims=True) + e_new
            o = (jnp.sum((e_buf / den) * vb, axis=0, keepdims=True) + (e_new / den) * v_new)
            o_g.append(o)
            l_g.append(mx + jnp.log(den))
        outs.append(o_g)
        lses.append(l_g)
    for h in range(H_A):
        l0, l1, l2 = lses[0][h], lses[1][h], lses[2][h]
        mx = jnp.maximum(jnp.maximum(l0, l1), l2)
        e0, e1, e2 = jnp.exp(l0 - mx), jnp.exp(l1 - mx), jnp.exp(l2 - mx)
        den = e0 + e1 + e2
        o_ref[0, :, h * HEAD_DIM:(h + 1) * HEAD_DIM] = (
            (e0 / den) * outs[0][h] + (e1 / den) * outs[1][h] + (e2 / den) * outs[2][h])


def _dilated_sample(qkv_s, bufs, layer, bias_a):
    db, n = qkv_s.shape
    hw = H_A * HEAD_DIM
    views, biases, specs = [], [], []
    for g, (win, dil) in enumerate(A_GROUPS):
        n_back = win // dil
        n_layers, _, lb = bufs[g].shape[:3]
        assert lb == win and lb % dil == 0, "window buffer must hold the full window"
        views.append(bufs[g].reshape(n_layers * db, n_back, dil, H_A, 2, HEAD_DIM))
        specs.append(pl.BlockSpec((None, n_back, None, H_A, 2, HEAD_DIM),
                                  lambda b: (layer * db + b, 0, 0, 0, 0, 0)))
        j = jnp.concatenate([n_back - jnp.arange(n_back), jnp.zeros((8,), jnp.int32)])
        biases.append(_bias_lookup(bias_a[:, g], j * dil).T)
    bias = jnp.concatenate(biases, axis=0)[:, :, None]
    out = pl.pallas_call(
        _dil_sample_kernel,
        grid=(db,),
        in_specs=[pl.BlockSpec((1, 1, n), lambda b: (b, 0, 0))] + specs
                 + [pl.BlockSpec(bias.shape, lambda b: (0, 0, 0))],
        out_specs=pl.BlockSpec((1, 1, hw), lambda b: (b, 0, 0)),
        out_shape=jax.ShapeDtypeStruct((db, 1, hw), F32),
        compiler_params=_params("arbitrary"),
        name="dilated_sample",
    )(qkv_s.reshape(db, 1, n), *views, bias)
    return out.reshape(db, hw)


def _heads_kernel(k_ref, v_ref, k3_ref, v3_ref):
    for c in range(H_B):
        sl = slice(c * HEAD_DIM, (c + 1) * HEAD_DIM)
        k3_ref[:, c, :] = k_ref[:, sl]
        v3_ref[:, c, :] = v_ref[:, sl]


def _kv_heads(qkv):
    s = qkv.shape[0]
    hw = H_B * HEAD_DIM
    tm = 512
    ospec = pl.BlockSpec((tm, H_B, HEAD_DIM), lambda i: (i, 0, 0))
    oshape = jax.ShapeDtypeStruct((s, H_B, HEAD_DIM), qkv.dtype)
    return pl.pallas_call(
        _heads_kernel,
        grid=(s // tm,),
        in_specs=[pl.BlockSpec((tm, hw), lambda i: (i, 1)), pl.BlockSpec((tm, hw), lambda i: (i, 2))],
        out_specs=[ospec, ospec],
        out_shape=[oshape, oshape],
        compiler_params=_params("arbitrary"),
        name="kv_heads",
    )(qkv, qkv)


def _kmean_kernel(k_ref, o_ref):
    for b in range(o_ref.shape[0]):
        o_ref[b:b + 1, :] = jnp.sum(k_ref[b * MOBA_BLOCK:(b + 1) * MOBA_BLOCK, :], axis=0,
                                    keepdims=True) / MOBA_BLOCK


def _moba_kmean(qkv):
    s = qkv.shape[0]
    hw = H_B * HEAD_DIM
    per = 8
    nb = s // MOBA_BLOCK
    return pl.pallas_call(
        _kmean_kernel,
        grid=(nb // per,),
        in_specs=[pl.BlockSpec((per * MOBA_BLOCK, hw), lambda i: (i, 1))],
        out_specs=pl.BlockSpec((per, hw), lambda i: (i, 0)),
        out_shape=jax.ShapeDtypeStruct((nb, hw), F32),
        compiler_params=_params("arbitrary"),
        name="moba_kmean",
    )(qkv)


def _top3_mask(sc, own):
    lane = lax.broadcasted_iota(jnp.int32, sc.shape, 1)
    lane_f = lane.astype(F32)
    past = lane < own
    cur = jnp.where(past, sc, NEG_INF)
    sel = lane == own
    for _ in range(MOBA_TOPK):
        mx = jnp.max(cur, axis=-1, keepdims=True)
        idx = jnp.min(jnp.where(cur == mx, lane_f, float(sc.shape[1])), axis=-1, keepdims=True)
        pick = lane_f == idx
        sel = sel | (pick & past)
        cur = jnp.where(pick, -jnp.inf, cur)
    return jnp.where(sel, 0.0, NEG_INF)


def _moba_select_kernel(q_ref, km_ref, o_ref, *, tq):
    q = q_ref[...]
    sc = lax.dot_general(q, km_ref[...], _TRANS_B, preferred_element_type=F32,
                         precision=lax.Precision.HIGHEST)
    pos = pl.program_id(1) * tq + lax.broadcasted_iota(jnp.int32, (tq, 1), 0)
    o_ref[:, :HEAD_DIM] = q.astype(BF16)
    o_ref[:, HEAD_DIM:] = _top3_mask(sc, pos // MOBA_BLOCK).astype(BF16)


def _moba_select(qkv, kmean):
    s = qkv.shape[0]
    nb = kmean.shape[0]
    assert nb <= HEAD_DIM
    tq = 1024
    km = jnp.pad(kmean, ((0, HEAD_DIM - nb), (0, 0)))
    return pl.pallas_call(
        functools.partial(_moba_select_kernel, tq=tq),
        grid=(H_B, s // tq),
        in_specs=[pl.BlockSpec((tq, HEAD_DIM), lambda h, t: (t, h)),
                  pl.BlockSpec((HEAD_DIM, HEAD_DIM), lambda h, t: (0, h))],
        out_specs=pl.BlockSpec((None, tq, 2 * HEAD_DIM), lambda h, t: (h, t, 0)),
        out_shape=jax.ShapeDtypeStruct((H_B, s, 2 * HEAD_DIM), BF16),
        compiler_params=_params("arbitrary", "arbitrary"),
        name="moba_select",
    )(qkv, km)


def _moba_attn_kernel(qx_ref, k_ref, v_ref, rev_ref, o_ref, kx_ref, vx_ref, t_ref, acc_ref, m_ref, l_ref, *, nb):
    blk = MOBA_BLOCK
    c1 = ATTN_SCALE * LOG2E
    lane = lax.broadcasted_iota(jnp.int32, (blk, HEAD_DIM), 1)
    row = lax.broadcasted_iota(jnp.int32, (blk, blk), 0)
    col = lax.broadcasted_iota(jnp.int32, (blk, blk), 1)

    def block_rows(b):
        return pl.ds(pl.multiple_of(b * blk, blk), blk)

    def fill(b, carry):
        rows = block_rows(b)
        kx_ref[rows, :HEAD_DIM] = k_ref[rows, :]
        kx_ref[rows, HEAD_DIM:] = jnp.where(lane == b, 1.0, 0.0).astype(BF16)
        vx_ref[rows, :HEAD_DIM] = v_ref[rows, :]
        vx_ref[rows, HEAD_DIM:] = jnp.ones((blk, HEAD_DIM), BF16)
        return carry
    lax.fori_loop(0, nb, fill, 0)

    def set_bias_tile(d):
        vec = jnp.concatenate([rev_ref[pl.ds(nb - d, 1), :], rev_ref[pl.ds(nb + 1 - d, 1), :]], axis=1)
        t = pltpu.roll(jnp.broadcast_to(vec, (blk, 2 * blk)), 0, 1, stride=1, stride_axis=0)[:, blk:]
        t_ref[...] = jnp.where(col <= row, t, NEG_INF) if isinstance(d, int) and d == 0 else t

    def scores(i, d):
        qrows, krows = block_rows(i), block_rows(i - d)
        s = lax.dot_general(qx_ref[qrows, :], kx_ref[krows, :], _TRANS_B, preferred_element_type=F32)
        return qrows, krows, s * c1 + t_ref[...]

    def first_tiles(ids, d):
        for i in ids:
            qrows, krows, s = scores(i, d)
            m = jnp.max(s, axis=-1, keepdims=True)
            pv = jnp.dot(jnp.exp2(s - m).astype(BF16), vx_ref[krows, :], preferred_element_type=F32)
            m_ref[qrows, :] = jnp.broadcast_to(m, (blk, HEAD_DIM))
            l_ref[qrows, :] = pv[:, HEAD_DIM:]
            acc_ref[qrows, :] = pv[:, :HEAD_DIM]

    def later_tiles(ids, d):
        loaded = []
        for i in ids:
            qrows, krows, s = scores(i, d)
            loaded.append((qrows, krows, s, m_ref[qrows, :], l_ref[qrows, :], acc_ref[qrows, :]))
        results = []
        for qrows, krows, s, m_old, l_old, acc_old in loaded:
            m_new = jnp.maximum(m_old, jnp.max(s, axis=-1, keepdims=True))
            alpha = jnp.exp2(m_old - m_new)
            p = jnp.exp2(s - jnp.concatenate([m_new] * (blk // HEAD_DIM), axis=1))
            pv = jnp.dot(p.astype(BF16), vx_ref[krows, :], preferred_element_type=F32)
            results.append((qrows, m_new, alpha * l_old + pv[:, HEAD_DIM:], alpha * acc_old + pv[:, :HEAD_DIM]))
        for qrows, m_new, l_new, acc_new in results:
            m_ref[qrows, :] = m_new
            l_ref[qrows, :] = l_new
            acc_ref[qrows, :] = acc_new

    u = MOBA_STREAMS
    assert u == 4

    def run(tiles, d):
        n_main = (nb - d) // u

        def main_body(it, carry):
            tiles([d + it * u + k for k in range(u)], d)
            return carry
        lax.fori_loop(0, n_main, main_body, 0)
        base = d + n_main * u
        rest = (nb - d) - n_main * u
        if isinstance(d, int) and (nb - d) % u == 0:
            return

        @pl.when(rest >= 2)
        def _():
            tiles([base, base + 1], d)

        @pl.when(rest % 2 == 1)
        def _():
            tiles([base + rest - 1], d)

    set_bias_tile(0)
    run(first_tiles, 0)

    def per_offset(d, carry):
        set_bias_tile(d)
        run(later_tiles, d)
        return carry
    lax.fori_loop(1, nb, per_offset, 0)

    def finish(b, carry):
        rows = block_rows(b)
        o_ref[rows, :] = (acc_ref[rows, :] / l_ref[rows, :]).astype(o_ref.dtype)
        return carry
    lax.fori_loop(0, nb, finish, 0)


def _moba_prompt(qkv, qkv_bf16, bias_b):
    s = qkv.shape[0]
    blk = MOBA_BLOCK
    nb = s // blk
    kmean = _moba_kmean(qkv)
    qx = _moba_select(qkv, kmean)
    dist = (nb + 1) * blk - jnp.arange((nb + 2) * blk)
    rev = (_bias_lookup(bias_b, dist) * LOG2E).T.reshape(H_B, nb + 2, blk)
    return pl.pallas_call(
        functools.partial(_moba_attn_kernel, nb=nb),
        grid=(H_B,),
        in_specs=[pl.BlockSpec((None, s, 2 * HEAD_DIM), lambda h: (h, 0, 0)),
                  pl.BlockSpec((s, HEAD_DIM), lambda h: (0, H_B + h)),
                  pl.BlockSpec((s, HEAD_DIM), lambda h: (0, 2 * H_B + h)),
                  pl.BlockSpec((None, nb + 2, blk), lambda h: (h, 0, 0))],
        out_specs=pl.BlockSpec((s, HEAD_DIM), lambda h: (0, h)),
        out_shape=jax.ShapeDtypeStruct((s, H_B * HEAD_DIM), BF16),
        scratch_shapes=[pltpu.VMEM((s, 2 * HEAD_DIM), BF16),
                        pltpu.VMEM((s, 2 * HEAD_DIM), BF16),
                        pltpu.VMEM((blk, blk), F32),
                        pltpu.VMEM((s, HEAD_DIM), F32),
                        pltpu.VMEM((s, HEAD_DIM), F32),
                        pltpu.VMEM((s, HEAD_DIM), F32)],
        compiler_params=_params("arbitrary"),
        name="moba_prompt",
    )(qx, qkv_bf16, qkv_bf16, rev)


def _page_sum_kernel(pt_ref, *refs, per):
    del pt_ref
    page_refs, o_ref = refs[:-1], refs[-1]
    step = pl.program_id(1)
    for t in range(len(page_refs) // per):
        acc = jnp.sum(page_refs[per * t][0], axis=0)
        for u in range(1, per):
            acc = acc + jnp.sum(page_refs[per * t + u][0], axis=0)
        o_ref[0, pl.ds(step * (len(page_refs) // per) + t, 1)] = acc[None]


def _moba_block_sums(cache_k, layer, page_table):
    n_layers, n_phys, page = cache_k.shape[:3]
    db, n_pages = page_table.shape
    per = MOBA_BLOCK // page
    nb = n_pages // per
    pages_per_step = 8
    assert n_pages % pages_per_step == 0 and pages_per_step % per == 0
    ck = cache_k.reshape(n_layers * n_phys, page, H_B, HEAD_DIM)

    def pspec(t):
        return pl.BlockSpec((1, page, H_B, HEAD_DIM),
                            lambda b, n, pt: (layer * n_phys + pt[b * n_pages + pages_per_step * n + t], 0, 0, 0))

    return pl.pallas_call(
        functools.partial(_page_sum_kernel, per=per),
        grid_spec=pltpu.PrefetchScalarGridSpec(
            num_scalar_prefetch=1,
            grid=(db, n_pages // pages_per_step),
            in_specs=[pspec(t) for t in range(pages_per_step)],
            out_specs=pl.BlockSpec((1, nb, H_B, HEAD_DIM), lambda b, n, pt: (b, 0, 0, 0))),
        out_shape=jax.ShapeDtypeStruct((db, nb, H_B, HEAD_DIM), F32),
        compiler_params=_params("arbitrary", "arbitrary"),
        name="moba_page_sums",
    )(page_table.reshape(-1), *([ck] * pages_per_step))


def _moba_sample_select_kernel(q_ref, ks_ref, o_ref, *, own):
    q = q_ref[0]
    nb = ks_ref.shape[1]
    lane = lax.broadcasted_iota(jnp.int32, (H_B, HEAD_DIM), 1)
    sc = jnp.zeros((H_B, HEAD_DIM), F32)
    for n in range(nb):
        kmean = ks_ref[0, n] / MOBA_BLOCK
        sc = jnp.where(lane == n, jnp.sum(q * kmean, axis=-1, keepdims=True), sc)
    mask = _top3_mask(sc, jnp.full((H_B, 1), own, jnp.int32))
    lane_f = lane.astype(F32)
    chosen = (mask == 0.0) & (lane < own)
    out = jnp.full((H_B, HEAD_DIM), -1.0, F32)
    cur = jnp.where(chosen, lane_f, float(HEAD_DIM))
    for r in range(MOBA_TOPK):
        idx = jnp.min(cur, axis=-1, keepdims=True)
        out = jnp.where(lane == r, jnp.where(idx < HEAD_DIM, idx, -1.0), out)
        cur = jnp.where(cur == idx, float(HEAD_DIM), cur)
    o_ref[0] = out.astype(jnp.int32)


def _moba_sample_select(q, ksum, own):
    db, nb = ksum.shape[:2]
    return pl.pallas_call(
        functools.partial(_moba_sample_select_kernel, own=own),
        grid=(db,),
        in_specs=[pl.BlockSpec((1, H_B, HEAD_DIM), lambda b: (b, 0, 0)),
                  pl.BlockSpec((1, nb, H_B, HEAD_DIM), lambda b: (b, 0, 0, 0))],
        out_specs=pl.BlockSpec((1, H_B, HEAD_DIM), lambda b: (b, 0, 0)),
        out_shape=jax.ShapeDtypeStruct((db, H_B, HEAD_DIM), jnp.int32),
        compiler_params=_params("arbitrary"),
        name="moba_sample_select",
    )(q, ksum)


def _moba_sample_attn_kernel(pages_ref, lpage_ref, q_ref, kn_ref, vn_ref, bias_ref, bias0_ref, k_hbm, v_hbm,
                             o_ref, kbuf, vbuf, sem, *, layer, n_slots):
    b = pl.program_id(0)
    slot = b % 2

    def copies(seq, buf):
        out = []
        for h in range(H_B):
            for t in range(n_slots):
                pg = pages_ref[(seq * H_B + h) * n_slots + t]
                out.append(pltpu.make_async_copy(k_hbm.at[layer, pg, :, h, :], kbuf.at[buf, h, t], sem.at[buf]))
                out.append(pltpu.make_async_copy(v_hbm.at[layer, pg, :, h, :], vbuf.at[buf, h, t], sem.at[buf]))
        return out

    @pl.when(b == 0)
    def _():
        for cp in copies(0, 0):
            cp.start()

    @pl.when(b + 1 < pl.num_programs(0))
    def _():
        for cp in copies(b + 1, 1 - slot):
            cp.start()

    for cp in copies(b, slot):
        cp.wait()
    lane = lax.broadcasted_iota(jnp.int32, bias_ref.shape[1:], 1)
    for h in range(H_B):
        q = q_ref[0, h:h + 1, :]
        s_list = []
        for t in range(n_slots):
            lp = lpage_ref[(b * H_B + h) * n_slots + t]
            s = jnp.sum(kbuf[slot, h, t] * q, axis=-1, keepdims=True) * ATTN_SCALE
            bias = jnp.sum(jnp.where(lane == lp, bias_ref[h], 0.0), axis=-1, keepdims=True)
            s_list.append(jnp.where(lp >= 0, s + bias, NEG_INF))
        s_new = jnp.sum(q * kn_ref[0, h:h + 1, :], axis=-1, keepdims=True) * ATTN_SCALE + bias0_ref[h]
        mx = s_new
        for s in s_list:
            mx = jnp.maximum(mx, jnp.max(s, axis=0, keepdims=True))
        e_new = jnp.exp(s_new - mx)
        e_list = [jnp.exp(s - mx) for s in s_list]
        den = e_new
        for e in e_list:
            den = den + jnp.sum(e, axis=0, keepdims=True)
        acc = (e_new / den) * vn_ref[0, h:h + 1, :]
        for t in range(n_slots):
            acc = acc + jnp.sum((e_list[t] / den) * vbuf[slot, h, t], axis=0, keepdims=True)
        o_ref[0, h:h + 1, :] = acc


def _moba_sample(qkv_s, cache_k, cache_v, layer, page_table, bias_b):
    db = qkv_s.shape[0]
    page = cache_k.shape[2]
    hw = H_B * HEAD_DIM
    n_pages = page_table.shape[1]
    past_len = n_pages * page
    assert past_len % MOBA_BLOCK == 0 and MOBA_BLOCK % page == 0
    own = past_len // MOBA_BLOCK
    per = MOBA_BLOCK // page
    n_slots = MOBA_TOPK * per
    q = qkv_s[:, :hw].reshape(db, H_B, HEAD_DIM)
    k_new = qkv_s[:, hw:2 * hw].reshape(db, H_B, HEAD_DIM)
    v_new = qkv_s[:, 2 * hw:].reshape(db, H_B, HEAD_DIM)

    ksum = _moba_block_sums(cache_k, layer, page_table)
    blocks = _moba_sample_select(q, ksum, own)[:, :, :MOBA_TOPK]
    lpage = jnp.where(blocks[..., None] >= 0, blocks[..., None] * per + jnp.arange(per), -1)
    lpage = lpage.reshape(db, H_B, n_slots)
    phys = jnp.take_along_axis(page_table[:, None, :], jnp.maximum(lpage, 0).reshape(db, 1, -1), axis=2)
    key_pos = jnp.arange(past_len).reshape(n_pages, page)
    bias_tbl = _bias_lookup(bias_b, past_len - key_pos).transpose(2, 1, 0)
    bias0 = bias_b[0].astype(F32).reshape(H_B, 1, 1)

    tok = pl.BlockSpec((1, H_B, HEAD_DIM), lambda b, pg, lp: (b, 0, 0))
    out = pl.pallas_call(
        functools.partial(_moba_sample_attn_kernel, layer=layer, n_slots=n_slots),
        grid_spec=pltpu.PrefetchScalarGridSpec(
            num_scalar_prefetch=2,
            grid=(db,),
            in_specs=[tok, tok, tok,
                      pl.BlockSpec((H_B, page, n_pages), lambda b, pg, lp: (0, 0, 0)),
                      pl.BlockSpec((H_B, 1, 1), lambda b, pg, lp: (0, 0, 0)),
                      pl.BlockSpec(memory_space=pl.ANY),
                      pl.BlockSpec(memory_space=pl.ANY)],
            out_specs=tok,
            scratch_shapes=[pltpu.VMEM((2, H_B, n_slots, page, HEAD_DIM), F32),
                            pltpu.VMEM((2, H_B, n_slots, page, HEAD_DIM), F32),
                            pltpu.SemaphoreType.DMA((2,))]),
        out_shape=jax.ShapeDtypeStruct((db, H_B, HEAD_DIM), F32),
        compiler_params=_params("arbitrary"),
        name="moba_sample_attn",
    )(phys.reshape(-1), lpage.reshape(-1), q, k_new, v_new, bias_tbl, bias0, cache_k, cache_v)
    return out.reshape(db, hw)


def kernel(x_prompt, x_sample, c_prompt, c_sample, cache_a_w128, cache_a_w512, cache_a_w2048, cache_b_k, cache_b_v, page_table, rel_bias, norm_g, w_ada, b_ada, w_ffn_in, w_ffn_out, w_qkv_a, qk_gain_a, w_o_a, w_qkv_b, qk_gain_b, w_o_b):
    batch, seq, d = x_prompt.shape
    db, dec_seq, _ = x_sample.shape
    assert batch == 1 and dec_seq == 1, "one prompt sequence, one new token per decode sequence"
    depth = norm_g.shape[0]
    n_ab = N_GROUPS_A * H_A
    bias_a = rel_bias[:, :n_ab].reshape(N_BUCKETS, N_GROUPS_A, H_A)
    bias_b = rel_bias[:, n_ab:]
    a_bufs = (cache_a_w128, cache_a_w512, cache_a_w2048)
    hw_b = H_B * HEAD_DIM

    rows = batch + db
    c_all = jnp.pad(jnp.concatenate([c_prompt, c_sample], axis=0), ((0, -rows % 8), (0, 0)))
    mod_all = _ada_all(c_all, w_ada, b_ada)

    xp = x_prompt.reshape(seq, d)
    xs = x_sample.reshape(db, d)
    a_new_p = [[] for _ in A_GROUPS]
    a_new_s = [[] for _ in A_GROUPS]
    bk_p, bk_s, bv_p, bv_s = [], [], [], []
    for layer in range(depth):
        mp = mod_all[layer, :batch]
        ms = mod_all[layer, batch:rows]
        hp, hs = _ffn_in(xp, xs, norm_g[layer, 0], mp, ms, 0, w_ffn_in, (layer, 0))
        xp, xs = _mm_res(hp, hs, w_ffn_out, (layer, 0), xp, xs, mp, ms, 0, FFN_RES)
        i = layer // 2
        if layer % 2 == 0:
            qkv_p, qkv_s = _qkv(xp, xs, norm_g[layer, 1], mp, ms, w_qkv_a, (i,), qk_gain_a)
            yp = _dilated_prompt(qkv_p, bias_a)
            ys = _dilated_sample(qkv_s, a_bufs, i, bias_a)
            w_o = w_o_a
            kv_p = qkv_p.reshape(seq, 3, N_GROUPS_A, H_A, HEAD_DIM)
            kv_s = qkv_s.reshape(db, 3, N_GROUPS_A, H_A, HEAD_DIM)
            for g, (win, dil) in enumerate(A_GROUPS):
                keep = min(win, seq)
                a_new_p[g].append(jnp.stack([kv_p[seq - keep:, 1, g], kv_p[seq - keep:, 2, g]], axis=2)[None])
                a_new_s[g].append(jnp.stack([kv_s[:, 1, g], kv_s[:, 2, g]], axis=2)[:, None])
        else:
            qkv_p, qkv_p16, qkv_s = _qkv(xp, xs, norm_g[layer, 1], mp, ms, w_qkv_b, (i,), qk_gain_b,
                                         with_bf16=True)
            yp = _moba_prompt(qkv_p, qkv_p16, bias_b)
            ys = _moba_sample(qkv_s, cache_b_k, cache_b_v, i, page_table, bias_b)
            w_o = w_o_b
            k_heads, v_heads = _kv_heads(qkv_p)
            bk_p.append(k_heads[None])
            bv_p.append(v_heads[None])
            bk_s.append(qkv_s[:, hw_b:2 * hw_b].reshape(db, dec_seq, H_B, HEAD_DIM))
            bv_s.append(qkv_s[:, 2 * hw_b:].reshape(db, dec_seq, H_B, HEAD_DIM))
        xp, xs = _mm_res(yp, ys, w_o, (i,), xp, xs, mp, ms, 1, 1.0)
        hp, hs = _ffn_in(xp, xs, norm_g[layer, 2], mp, ms, 2, w_ffn_in, (layer, 1))
        xp, xs = _mm_res(hp, hs, w_ffn_out, (layer, 1), xp, xs, mp, ms, 2, FFN_RES)
    return (xp.reshape(batch, seq, d), xs.reshape(db, dec_seq, d),
            jnp.stack(a_new_p[0]), jnp.stack(a_new_s[0]), jnp.stack(a_new_p[1]), jnp.stack(a_new_s[1]),
            jnp.stack(a_new_p[2]), jnp.stack(a_new_s[2]),
            jnp.stack(bk_p), jnp.stack(bk_s), jnp.stack(bv_p), jnp.stack(bv_s))
```

```python
import functools
import math

import jax
import jax.numpy as jnp
from jax import lax
from jax.experimental import pallas as pl
from jax.experimental.pallas import tpu as pltpu

F32 = jnp.float32
BF16 = jnp.bfloat16

D_MODEL = 2048
HEAD_DIM = 128
A_GROUPS = ((128, 1), (512, 4), (2048, 16))
N_GROUPS_A = 3
H_A = 8
A_Q_BLOCK = 128
H_B = 16
MOBA_BLOCK = 256
MOBA_TOPK = 3
N_BUCKETS = 32
REL_MAX_DIST = 4096
D_FF = 5632
FFN_RES = 0.5
RMS_EPS = 1e-6
NEG_INF = -1e30
ATTN_SCALE = HEAD_DIM ** -0.5
LOG2E = math.log2(math.e)

VMEM_LIMIT_BYTES = 56 * 1024 * 1024
ROW_TILE = 1024
A_SUPER = max(dil for _, dil in A_GROUPS) * A_Q_BLOCK
MOBA_STREAMS = 4

_TRANS_B = (((1,), (1,)), ((), ()))


def _params(*sem):
    return pltpu.CompilerParams(dimension_semantics=sem, vmem_limit_bytes=VMEM_LIMIT_BYTES)


def _t5_bucket(dist):
    n = jnp.maximum(dist, 0)
    max_exact = N_BUCKETS // 2
    nf = jnp.maximum(n, 1).astype(F32)
    large = max_exact + (jnp.log(nf / max_exact) / math.log(REL_MAX_DIST / max_exact)
                         * (N_BUCKETS - max_exact)).astype(jnp.int32)
    large = jnp.minimum(large, N_BUCKETS - 1)
    return jnp.where(n < max_exact, n, large)


def _bias_lookup(table, dist):
    onehot = (_t5_bucket(dist)[..., None] == jnp.arange(N_BUCKETS)).astype(F32)
    return jnp.einsum('...b,bh->...h', onehot, table.astype(F32), precision=lax.Precision.HIGHEST)


def _silu(x):
    return x * jax.nn.sigmoid(x)


def _norm_mod(x, g, scale, shift):
    y = x * lax.rsqrt(jnp.mean(x * x, axis=-1, keepdims=True) + RMS_EPS)
    return (y * g) * (1.0 + scale) + shift


def _ada_kernel(c_ref, w_ref, b_ref, o_ref):
    a = _silu(c_ref[...]).astype(BF16)
    o_ref[...] = jnp.dot(a, w_ref[...].astype(BF16), preferred_element_type=F32) + b_ref[...]


def _ada_all(c_all, w_ada, b_ada):
    depth, d, n = w_ada.shape
    r = c_all.shape[0]
    tn = 1024
    return pl.pallas_call(
        _ada_kernel,
        grid=(depth, n // tn),
        in_specs=[pl.BlockSpec((r, d), lambda l, j: (0, 0)),
                  pl.BlockSpec((None, d, tn), lambda l, j: (l, 0, j)),
                  pl.BlockSpec((None, 1, tn), lambda l, j: (l, 0, j))],
        out_specs=pl.BlockSpec((None, r, tn), lambda l, j: (l, 0, j)),
        out_shape=jax.ShapeDtypeStruct((depth, r, n), F32),
        compiler_params=_params("arbitrary", "arbitrary"),
        name="ada_mod",
    )(c_all, w_ada, b_ada.reshape(depth, 1, n))


def _on_last_row_tile():
    return pl.program_id(0) == pl.num_programs(0) - 1


def _ffn_in_kernel(x_ref, g_ref, sc_ref, sh_ref, xs_ref, scs_ref, shs_ref, wg_ref, wu_ref, o_ref, os_ref,
                   h_ref, hs_ref):
    first_col = pl.program_id(1) == 0

    @pl.when(first_col)
    def _():
        h_ref[...] = _norm_mod(x_ref[...], g_ref[...], sc_ref[...], sh_ref[...]).astype(BF16)

    wg = wg_ref[...].astype(BF16)
    wu = wu_ref[...].astype(BF16)

    def swiglu(h):
        a = jnp.dot(h, wg, preferred_element_type=F32)
        u = jnp.dot(h, wu, preferred_element_type=F32)
        return (_silu(a) * u).astype(BF16)

    o_ref[...] = swiglu(h_ref[...])

    @pl.when(_on_last_row_tile())
    def _():
        @pl.when(first_col)
        def _():
            hs_ref[...] = _norm_mod(xs_ref[...], g_ref[...], scs_ref[...], shs_ref[...]).astype(BF16)

        os_ref[...] = swiglu(hs_ref[...])


def _lead_spec(w, lead, block, index):
    assert w.ndim == len(lead) + len(block)
    return pl.BlockSpec((None,) * len(lead) + block, lambda i, j: lead + index(i, j))


def _decode_col(n_row_tiles):
    return lambda i, j: jnp.where(i == n_row_tiles - 1, j, 0)


def _ffn_in(xp, xs, g, mp, ms, sub, w_in, lead):
    m, d = xp.shape
    ns = xs.shape[0]
    f = w_in.shape[-1] // 2
    tm = ROW_TILE
    tn = 512
    nj = f // tn
    sj = _decode_col(m // tm)
    return pl.pallas_call(
        _ffn_in_kernel,
        grid=(m // tm, nj),
        in_specs=[pl.BlockSpec((tm, d), lambda i, j: (i, 0)),
                  pl.BlockSpec((1, d), lambda i, j: (0, 0)),
                  pl.BlockSpec((1, d), lambda i, j: (0, 3 * sub + 1)),
                  pl.BlockSpec((1, d), lambda i, j: (0, 3 * sub)),
                  pl.BlockSpec((ns, d), lambda i, j: (0, 0)),
                  pl.BlockSpec((ns, d), lambda i, j: (0, 3 * sub + 1)),
                  pl.BlockSpec((ns, d), lambda i, j: (0, 3 * sub)),
                  _lead_spec(w_in, lead, (d, tn), lambda i, j: (0, j)),
                  _lead_spec(w_in, lead, (d, tn), lambda i, j: (0, j + nj))],
        out_specs=[pl.BlockSpec((tm, tn), lambda i, j: (i, j)),
                   pl.BlockSpec((ns, tn), lambda i, j: (0, sj(i, j)))],
        out_shape=[jax.ShapeDtypeStruct((m, f), BF16), jax.ShapeDtypeStruct((ns, f), BF16)],
        scratch_shapes=[pltpu.VMEM((tm, d), BF16), pltpu.VMEM((ns, d), BF16)],
        compiler_params=_params("arbitrary", "arbitrary"),
        name="ffn_in",
    )(xp, g.reshape(1, d), mp, mp, xs, ms, ms, w_in, w_in)


def _mm_res_kernel(a_ref, x_ref, gate_ref, as_ref, xs_ref, gates_ref, w_ref, o_ref, os_ref, *, coef):
    w = w_ref[...].astype(BF16)
    acc = jnp.dot(a_ref[...].astype(BF16), w, preferred_element_type=F32)
    o_ref[...] = x_ref[...] + (coef * gate_ref[...]) * acc

    @pl.when(_on_last_row_tile())
    def _():
        acc_s = jnp.dot(as_ref[...].astype(BF16), w, preferred_element_type=F32)
        os_ref[...] = xs_ref[...] + (coef * gates_ref[...]) * acc_s


def _mm_res(ap, as_, w, lead, xp, xs, mp, ms, sub, coef):
    m, k = ap.shape
    ns = as_.shape[0]
    n = w.shape[-1]
    tm = ROW_TILE
    tn = 256
    per = n // tn
    sj = _decode_col(m // tm)
    return pl.pallas_call(
        functools.partial(_mm_res_kernel, coef=coef),
        grid=(m // tm, n // tn),
        in_specs=[pl.BlockSpec((tm, k), lambda i, j: (i, 0)),
                  pl.BlockSpec((tm, tn), lambda i, j: (i, j)),
                  pl.BlockSpec((1, tn), lambda i, j: (0, (3 * sub + 2) * per + j)),
                  pl.BlockSpec((ns, k), lambda i, j: (0, 0)),
                  pl.BlockSpec((ns, tn), lambda i, j: (0, sj(i, j))),
                  pl.BlockSpec((ns, tn), lambda i, j: (0, (3 * sub + 2) * per + sj(i, j))),
                  _lead_spec(w, lead, (k, tn), lambda i, j: (0, j))],
        out_specs=[pl.BlockSpec((tm, tn), lambda i, j: (i, j)),
                   pl.BlockSpec((ns, tn), lambda i, j: (0, sj(i, j)))],
        out_shape=[jax.ShapeDtypeStruct((m, n), F32), jax.ShapeDtypeStruct((ns, n), F32)],
        compiler_params=_params("arbitrary", "arbitrary"),
        name="mm_res",
    )(ap, xp, mp, as_, xs, ms, w)


def _qkv_kernel(x_ref, g_ref, sc_ref, sh_ref, xs_ref, scs_ref, shs_ref, w_ref, gain_ref, *rest,
                tiles_per_part, with_bf16):
    o_ref, o16_ref = rest[0], (rest[1] if with_bf16 else None)
    os_ref, h_ref, hs_ref = rest[-3:]
    j = pl.program_id(1)
    part = j // tiles_per_part

    @pl.when(j == 0)
    def _():
        h_ref[...] = _norm_mod(x_ref[...], g_ref[...], sc_ref[...], sh_ref[...]).astype(BF16)

    half = w_ref.shape[1] // 2
    w_halves = [w_ref[:, c * half:(c + 1) * half].astype(BF16) for c in range(2)]
    gain = gain_ref[pl.ds(jnp.minimum(part, 1), 1), :]
    normed = part < 2

    def emit(h, out_ref, out16_ref):
        accs = [jnp.dot(h, w, preferred_element_type=F32) for w in w_halves]
        for c2, acc in enumerate(accs):
            for c in range(half // HEAD_DIM):
                a = acc[:, c * HEAD_DIM:(c + 1) * HEAD_DIM]
                y = (a * lax.rsqrt(jnp.mean(a * a, axis=-1, keepdims=True) + RMS_EPS)) * gain
                y = jnp.where(normed, y, a)
                sl = slice(c2 * half + c * HEAD_DIM, c2 * half + (c + 1) * HEAD_DIM)
                out_ref[:, sl] = y
                if out16_ref is not None:
                    out16_ref[:, sl] = y.astype(BF16)

    emit(h_ref[...], o_ref, o16_ref)

    @pl.when(_on_last_row_tile())
    def _():
        @pl.when(j == 0)
        def _():
            hs_ref[...] = _norm_mod(xs_ref[...], g_ref[...], scs_ref[...], shs_ref[...]).astype(BF16)

        emit(hs_ref[...], os_ref, None)


def _qkv(xp, xs, g, mp, ms, w_qkv, lead, qk_gain, with_bf16=False):
    m, d = xp.shape
    ns = xs.shape[0]
    n = w_qkv.shape[-1]
    tm = ROW_TILE
    tn = 512
    sj = _decode_col(m // tm)
    ospec = pl.BlockSpec((tm, tn), lambda i, j: (i, j))
    out_shape = [jax.ShapeDtypeStruct((m, n), F32)]
    if with_bf16:
        out_shape.append(jax.ShapeDtypeStruct((m, n), BF16))
    out_specs = [ospec] * len(out_shape) + [pl.BlockSpec((ns, tn), lambda i, j: (0, sj(i, j)))]
    out_shape.append(jax.ShapeDtypeStruct((ns, n), F32))
    return pl.pallas_call(
        functools.partial(_qkv_kernel, tiles_per_part=n // 3 // tn, with_bf16=with_bf16),
        grid=(m // tm, n // tn),
        in_specs=[pl.BlockSpec((tm, d), lambda i, j: (i, 0)),
                  pl.BlockSpec((1, d), lambda i, j: (0, 0)),
                  pl.BlockSpec((1, d), lambda i, j: (0, 4)),
                  pl.BlockSpec((1, d), lambda i, j: (0, 3)),
                  pl.BlockSpec((ns, d), lambda i, j: (0, 0)),
                  pl.BlockSpec((ns, d), lambda i, j: (0, 4)),
                  pl.BlockSpec((ns, d), lambda i, j: (0, 3)),
                  _lead_spec(w_qkv, lead, (d, tn), lambda i, j: (0, j)),
                  _lead_spec(qk_gain, lead, (2, HEAD_DIM), lambda i, j: (0, 0))],
        out_specs=out_specs,
        out_shape=out_shape,
        scratch_shapes=[pltpu.VMEM((tm, d), BF16), pltpu.VMEM((ns, d), BF16)],
        compiler_params=_params("arbitrary", "arbitrary"),
        name="qkv",
    )(xp, g.reshape(1, d), mp, mp, xs, ms, ms, w_qkv, qk_gain)


def _dil_kernel(*refs):
    n_in = 5 * N_GROUPS_A
    bias_ref, o_ref, og_ref, lg_ref = refs[n_in:]
    qb = A_Q_BLOCK
    sb = pl.program_id(0)
    col = lax.broadcasted_iota(jnp.int32, (qb, 2 * qb), 1)
    prev_exists = (sb > 0) | (col >= qb)
    for g, (_, dil) in enumerate(A_GROUPS):
        q_ref, kc_ref, kp_ref, vc_ref, vp_ref = refs[5 * g:5 * g + 5]
        nblk = A_SUPER // (dil * qb)
        bias = bias_ref[g]

        def rows_of(n, r):
            return pl.ds(n * qb * dil + r, qb, stride=dil) if dil > 1 else pl.ds(n * qb, qb)

        for r in range(dil):
            for n in range(nblk):
                rows = rows_of(n, r)
                if n > 0:
                    k_prev, v_prev = kc_ref[rows_of(n - 1, r), :], vc_ref[rows_of(n - 1, r), :]
                else:
                    k_prev, v_prev = kp_ref[rows_of(0, r), :], vp_ref[rows_of(0, r), :]
                q = q_ref[rows, :].astype(BF16)
                kk = jnp.concatenate([k_prev, kc_ref[rows, :]], axis=0).astype(BF16)
                vv = jnp.concatenate([v_prev, vc_ref[rows, :]], axis=0).astype(BF16)
                s = lax.dot_general(q, kk, _TRANS_B, preferred_element_type=F32) * ATTN_SCALE + bias
                if n == 0:
                    s = jnp.where(prev_exists, s, NEG_INF)
                mx = jnp.max(s, axis=-1, keepdims=True)
                e = jnp.exp(s - mx)
                den = jnp.sum(e, axis=-1, keepdims=True)
                og_ref[g, rows, :] = jnp.dot((e / den).astype(BF16), vv, preferred_element_type=F32)
                lg_ref[g, rows, :] = mx + jnp.log(den)
    l0, l1, l2 = lg_ref[0], lg_ref[1], lg_ref[2]
    mx = jnp.maximum(jnp.maximum(l0, l1), l2)
    e0, e1, e2 = jnp.exp(l0 - mx), jnp.exp(l1 - mx), jnp.exp(l2 - mx)
    den = e0 + e1 + e2
    o_ref[...] = ((e0 / den) * og_ref[0] + (e1 / den) * og_ref[1] + (e2 / den) * og_ref[2]).astype(o_ref.dtype)


def _dilated_prompt(qkv, bias_a):
    s, n = qkv.shape
    qb = A_Q_BLOCK
    assert s % A_SUPER == 0
    gh = N_GROUPS_A * H_A
    qi = qb + jnp.arange(qb)
    kj = jnp.arange(2 * qb)
    diff = qi[:, None] - kj[None, :]
    tables = []
    for g, (win, dil) in enumerate(A_GROUPS):
        band = (diff >= 0) & (diff <= win // dil)
        bias = _bias_lookup(bias_a[:, g], diff * dil).transpose(2, 0, 1)
        tables.append(jnp.where(band[None], bias, NEG_INF))
    bias = jnp.stack(tables)

    in_specs, operands = [], []
    for g, (_, dil) in enumerate(A_GROUPS):
        span = dil * qb
        per_super = A_SUPER // span
        for part, prev in ((0, False), (1, False), (1, True), (2, False), (2, True)):
            cb = part * gh + g * H_A
            if prev:
                in_specs.append(pl.BlockSpec(
                    (span, HEAD_DIM),
                    lambda sb, h, cb=cb, per_super=per_super: (jnp.maximum(sb * per_super - 1, 0), cb + h)))
            else:
                in_specs.append(pl.BlockSpec((A_SUPER, HEAD_DIM), lambda sb, h, cb=cb: (sb, cb + h)))
            operands.append(qkv)
    in_specs.append(pl.BlockSpec((N_GROUPS_A, None, qb, 2 * qb), lambda sb, h: (0, h, 0, 0)))
    return pl.pallas_call(
        _dil_kernel,
        grid=(s // A_SUPER, H_A),
        in_specs=in_specs,
        out_specs=pl.BlockSpec((A_SUPER, HEAD_DIM), lambda sb, h: (sb, h)),
        out_shape=jax.ShapeDtypeStruct((s, H_A * HEAD_DIM), BF16),
        scratch_shapes=[pltpu.VMEM((N_GROUPS_A, A_SUPER, HEAD_DIM), F32),
                        pltpu.VMEM((N_GROUPS_A, A_SUPER, 1), F32)],
        compiler_params=_params("arbitrary", "arbitrary"),
        name="dilated_prompt",
    )(*operands, bias)


def _dil_sample_kernel(q_ref, b0_ref, b1_ref, b2_ref, bias_ref, o_ref):
    hw = H_A * HEAD_DIM
    row = q_ref[0]
    outs, lses = [], []
    for g, buf_ref in enumerate((b0_ref, b1_ref, b2_ref)):
        o_g, l_g = [], []
        for h in range(H_A):
            c = g * hw + h * HEAD_DIM
            q = row[:, c:c + HEAD_DIM]
            k_new = row[:, 3 * hw + c:3 * hw + c + HEAD_DIM]
            v_new = row[:, 6 * hw + c:6 * hw + c + HEAD_DIM]
            kb = buf_ref[:, h, 0, :]
            vb = buf_ref[:, h, 1, :]
            bias = bias_ref[g * H_A + h]
            nb = kb.shape[0]
            s_buf = jnp.sum(kb * q, axis=-1, keepdims=True) * ATTN_SCALE + bias[:nb]
            s_new = jnp.sum(k_new * q, axis=-1, keepdims=True) * ATTN_SCALE + bias[nb:nb + 1]
            mx = jnp.maximum(jnp.max(s_buf, axis=0, keepdims=True), s_new)
            e_buf = jnp.exp(s_buf - mx)
            e_new = jnp.exp(s_new - mx)
            den = jnp.sum(e_buf, axis=0, keepdims=True) + e_new
            o = (jnp.sum((e_buf / den) * vb, axis=0, keepdims=True) + (e_new / den) * v_new)
            o_g.append(o)
            l_g.append(mx + jnp.log(den))
        outs.append(o_g)
        lses.append(l_g)
    for h in range(H_A):
        l0, l1, l2 = lses[0][h], lses[1][h], lses[2][h]
        mx = jnp.maximum(jnp.maximum(l0, l1), l2)
        e0, e1, e2 = jnp.exp(l0 - mx), jnp.exp(l1 - mx), jnp.exp(l2 - mx)
        den = e0 + e1 + e2
        o_ref[0, :, h * HEAD_DIM:(h + 1) * HEAD_DIM] = (
            (e0 / den) * outs[0][h] + (e1 / den) * outs[1][h] + (e2 / den) * outs[2][h])


def _dilated_sample(qkv_s, bufs, layer, bias_a):
    db, n = qkv_s.shape
    hw = H_A * HEAD_DIM
    views, biases, specs = [], [], []
    for g, (win, dil) in enumerate(A_GROUPS):
        n_back = win // dil
        n_layers, _, lb = bufs[g].shape[:3]
        assert lb == win and lb % dil == 0, "window buffer must hold the full window"
        views.append(bufs[g].reshape(n_layers * db, n_back, dil, H_A, 2, HEAD_DIM))
        specs.append(pl.BlockSpec((None, n_back, None, H_A, 2, HEAD_DIM),
                                  lambda b: (layer * db + b, 0, 0, 0, 0, 0)))
        j = jnp.concatenate([n_back - jnp.arange(n_back), jnp.zeros((8,), jnp.int32)])
        biases.append(_bias_lookup(bias_a[:, g], j * dil).T)
    bias = jnp.concatenate(biases, axis=0)[:, :, None]
    out = pl.pallas_call(
        _dil_sample_kernel,
        grid=(db,),
        in_specs=[pl.BlockSpec((1, 1, n), lambda b: (b, 0, 0))] + specs
                 + [pl.BlockSpec(bias.shape, lambda b: (0, 0, 0))],
        out_specs=pl.BlockSpec((1, 1, hw), lambda b: (b, 0, 0)),
        out_shape=jax.ShapeDtypeStruct((db, 1, hw), F32),
        compiler_params=_params("arbitrary"),
        name="dilated_sample",
    )(qkv_s.reshape(db, 1, n), *views, bias)
    return out.reshape(db, hw)


def _heads_kernel(k_ref, v_ref, k3_ref, v3_ref):
    for c in range(H_B):
        sl = slice(c * HEAD_DIM, (c + 1) * HEAD_DIM)
        k3_ref[:, c, :] = k_ref[:, sl]
        v3_ref[:, c, :] = v_ref[:, sl]


def _kv_heads(qkv):
    s = qkv.shape[0]
    hw = H_B * HEAD_DIM
    tm = 512
    ospec = pl.BlockSpec((tm, H_B, HEAD_DIM), lambda i: (i, 0, 0))
    oshape = jax.ShapeDtypeStruct((s, H_B, HEAD_DIM), qkv.dtype)
    return pl.pallas_call(
        _heads_kernel,
        grid=(s // tm,),
        in_specs=[pl.BlockSpec((tm, hw), lambda i: (i, 1)), pl.BlockSpec((tm, hw), lambda i: (i, 2))],
        out_specs=[ospec, ospec],
        out_shape=[oshape, oshape],
        compiler_params=_params("arbitrary"),
        name="kv_heads",
    )(qkv, qkv)


def _kmean_kernel(k_ref, o_ref):
    for b in range(o_ref.shape[0]):
        o_ref[b:b + 1, :] = jnp.sum(k_ref[b * MOBA_BLOCK:(b + 1) * MOBA_BLOCK, :], axis=0,
                                    keepdims=True) / MOBA_BLOCK


def _moba_kmean(qkv):
    s = qkv.shape[0]
    hw = H_B * HEAD_DIM
    per = 8
    nb = s // MOBA_BLOCK
    return pl.pallas_call(
        _kmean_kernel,
        grid=(nb // per,),
        in_specs=[pl.BlockSpec((per * MOBA_BLOCK, hw), lambda i: (i, 1))],
        out_specs=pl.BlockSpec((per, hw), lambda i: (i, 0)),
        out_shape=jax.ShapeDtypeStruct((nb, hw), F32),
        compiler_params=_params("arbitrary"),
        name="moba_kmean",
    )(qkv)


def _top3_mask(sc, own, axis):
    blk = lax.broadcasted_iota(jnp.int32, sc.shape, axis)
    blk_f = blk.astype(F32)
    past = blk < own
    cur = jnp.where(past, sc, NEG_INF)
    sel = blk == own
    for _ in range(MOBA_TOPK):
        mx = jnp.max(cur, axis=axis, keepdims=True)
        idx = jnp.min(jnp.where(cur == mx, blk_f, float(sc.shape[axis])), axis=axis, keepdims=True)
        pick = blk_f == idx
        sel = sel | (pick & past)
        cur = jnp.where(pick, -jnp.inf, cur)
    return jnp.where(sel, 0.0, NEG_INF)


def _moba_select_kernel(q_ref, km_ref, o_ref, *, tq):
    q = q_ref[...]
    sc = lax.dot_general(km_ref[...], q, _TRANS_B, preferred_element_type=F32,
                         precision=lax.Precision.HIGHEST)
    pos = pl.program_id(1) * tq + lax.broadcasted_iota(jnp.int32, (1, tq), 1)
    mask = _top3_mask(sc, pos // MOBA_BLOCK, axis=0)
    mask = jnp.concatenate([mask, jnp.zeros((HEAD_DIM - mask.shape[0], tq), F32)], axis=0)
    o_ref[:, :HEAD_DIM] = q.astype(BF16)
    o_ref[:, HEAD_DIM:] = mask.T.astype(BF16)


def _moba_select(qkv, kmean):
    s = qkv.shape[0]
    nb = kmean.shape[0]
    assert nb <= HEAD_DIM and nb % 8 == 0
    tq = 1024
    return pl.pallas_call(
        functools.partial(_moba_select_kernel, tq=tq),
        grid=(H_B, s // tq),
        in_specs=[pl.BlockSpec((tq, HEAD_DIM), lambda h, t: (t, h)),
                  pl.BlockSpec((nb, HEAD_DIM), lambda h, t: (0, h))],
        out_specs=pl.BlockSpec((None, tq, 2 * HEAD_DIM), lambda h, t: (h, t, 0)),
        out_shape=jax.ShapeDtypeStruct((H_B, s, 2 * HEAD_DIM), BF16),
        compiler_params=_params("arbitrary", "arbitrary"),
        name="moba_select",
    )(qkv, kmean)


def _moba_attn_kernel(qx_ref, k_ref, v_ref, rev_ref, o_ref, kx_ref, vx_ref, t_ref, acc_ref, m_ref, l_ref, *, nb):
    blk = MOBA_BLOCK
    c1 = ATTN_SCALE * LOG2E
    lane = lax.broadcasted_iota(jnp.int32, (blk, HEAD_DIM), 1)
    row = lax.broadcasted_iota(jnp.int32, (blk, blk), 0)
    col = lax.broadcasted_iota(jnp.int32, (blk, blk), 1)

    def block_rows(b):
        return pl.ds(pl.multiple_of(b * blk, blk), blk)

    def fill(b, carry):
        rows = block_rows(b)
        kx_ref[rows, :HEAD_DIM] = k_ref[rows, :]
        kx_ref[rows, HEAD_DIM:] = jnp.where(lane == b, 1.0, 0.0).astype(BF16)
        vx_ref[rows, :HEAD_DIM] = v_ref[rows, :]
        vx_ref[rows, HEAD_DIM:] = jnp.ones((blk, HEAD_DIM), BF16)
        return carry
    lax.fori_loop(0, nb, fill, 0)

    def set_bias_tile(d):
        vec = jnp.concatenate([rev_ref[pl.ds(nb - d, 1), :], rev_ref[pl.ds(nb + 1 - d, 1), :]], axis=1)
        t = pltpu.roll(jnp.broadcast_to(vec, (blk, 2 * blk)), 0, 1, stride=1, stride_axis=0)[:, blk:]
        t_ref[...] = jnp.where(col <= row, t, NEG_INF) if isinstance(d, int) and d == 0 else t

    def scores(i, d):
        qrows, krows = block_rows(i), block_rows(i - d)
        s = lax.dot_general(qx_ref[qrows, :], kx_ref[krows, :], _TRANS_B, preferred_element_type=F32)
        return qrows, krows, s * c1 + t_ref[...]

    def first_tiles(ids, d):
        for i in ids:
            qrows, krows, s = scores(i, d)
            m = jnp.max(s, axis=-1, keepdims=True)
            pv = jnp.dot(jnp.exp2(s - m).astype(BF16), vx_ref[krows, :], preferred_element_type=F32)
            m_ref[qrows, :] = jnp.broadcast_to(m, (blk, HEAD_DIM))
            l_ref[qrows, :] = pv[:, HEAD_DIM:]
            acc_ref[qrows, :] = pv[:, :HEAD_DIM]

    def later_tiles(ids, d):
        loaded = []
        for i in ids:
            qrows, krows, s = scores(i, d)
            loaded.append((qrows, krows, s, m_ref[qrows, :], l_ref[qrows, :], acc_ref[qrows, :]))
        results = []
        for qrows, krows, s, m_old, l_old, acc_old in loaded:
            m_new = jnp.maximum(m_old, jnp.max(s, axis=-1, keepdims=True))
            alpha = jnp.exp2(m_old - m_new)
            p = jnp.exp2(s - jnp.concatenate([m_new] * (blk // HEAD_DIM), axis=1))
            pv = jnp.dot(p.astype(BF16), vx_ref[krows, :], preferred_element_type=F32)
            results.append((qrows, m_new, alpha * l_old + pv[:, HEAD_DIM:], alpha * acc_old + pv[:, :HEAD_DIM]))
        for qrows, m_new, l_new, acc_new in results:
            m_ref[qrows, :] = m_new
            l_ref[qrows, :] = l_new
            acc_ref[qrows, :] = acc_new

    u = MOBA_STREAMS
    assert u == 4

    def run(tiles, d):
        n_main = (nb - d) // u

        def main_body(it, carry):
            tiles([d + it * u + k for k in range(u)], d)
            return carry
        lax.fori_loop(0, n_main, main_body, 0)
        base = d + n_main * u
        rest = (nb - d) - n_main * u
        if isinstance(d, int) and (nb - d) % u == 0:
            return

        @pl.when(rest >= 2)
        def _():
            tiles([base, base + 1], d)

        @pl.when(rest % 2 == 1)
        def _():
            tiles([base + rest - 1], d)

    set_bias_tile(0)
    run(first_tiles, 0)

    def per_offset(d, carry):
        set_bias_tile(d)
        run(later_tiles, d)
        return carry
    lax.fori_loop(1, nb, per_offset, 0)

    def finish(b, carry):
        rows = block_rows(b)
        o_ref[rows, :] = (acc_ref[rows, :] / l_ref[rows, :]).astype(o_ref.dtype)
        return carry
    lax.fori_loop(0, nb, finish, 0)


def _moba_prompt(qkv, qkv_bf16, bias_b):
    s = qkv.shape[0]
    blk = MOBA_BLOCK
    nb = s // blk
    kmean = _moba_kmean(qkv)
    qx = _moba_select(qkv, kmean)
    dist = (nb + 1) * blk - jnp.arange((nb + 2) * blk)
    rev = (_bias_lookup(bias_b, dist) * LOG2E).T.reshape(H_B, nb + 2, blk)
    return pl.pallas_call(
        functools.partial(_moba_attn_kernel, nb=nb),
        grid=(H_B,),
        in_specs=[pl.BlockSpec((None, s, 2 * HEAD_DIM), lambda h: (h, 0, 0)),
                  pl.BlockSpec((s, HEAD_DIM), lambda h: (0, H_B + h)),
                  pl.BlockSpec((s, HEAD_DIM), lambda h: (0, 2 * H_B + h)),
                  pl.BlockSpec((None, nb + 2, blk), lambda h: (h, 0, 0))],
        out_specs=pl.BlockSpec((s, HEAD_DIM), lambda h: (0, h)),
        out_shape=jax.ShapeDtypeStruct((s, H_B * HEAD_DIM), BF16),
        scratch_shapes=[pltpu.VMEM((s, 2 * HEAD_DIM), BF16),
                        pltpu.VMEM((s, 2 * HEAD_DIM), BF16),
                        pltpu.VMEM((blk, blk), F32),
                        pltpu.VMEM((s, HEAD_DIM), F32),
                        pltpu.VMEM((s, HEAD_DIM), F32),
                        pltpu.VMEM((s, HEAD_DIM), F32)],
        compiler_params=_params("arbitrary"),
        name="moba_prompt",
    )(qx, qkv_bf16, qkv_bf16, rev)


def _page_sum_kernel(pt_ref, *refs, per):
    del pt_ref
    page_refs, o_ref = refs[:-1], refs[-1]
    step = pl.program_id(1)
    for t in range(len(page_refs) // per):
        acc = jnp.sum(page_refs[per * t][0], axis=0)
        for u in range(1, per):
            acc = acc + jnp.sum(page_refs[per * t + u][0], axis=0)
        o_ref[0, pl.ds(step * (len(page_refs) // per) + t, 1)] = acc[None]


def _moba_block_sums(cache_k, layer, page_table):
    n_layers, n_phys, page = cache_k.shape[:3]
    db, n_pages = page_table.shape
    per = MOBA_BLOCK // page
    nb = n_pages // per
    pages_per_step = 8
    assert n_pages % pages_per_step == 0 and pages_per_step % per == 0
    ck = cache_k.reshape(n_layers * n_phys, page, H_B, HEAD_DIM)

    def pspec(t):
        return pl.BlockSpec((1, page, H_B, HEAD_DIM),
                            lambda b, n, pt: (layer * n_phys + pt[b * n_pages + pages_per_step * n + t], 0, 0, 0))

    return pl.pallas_call(
        functools.partial(_page_sum_kernel, per=per),
        grid_spec=pltpu.PrefetchScalarGridSpec(
            num_scalar_prefetch=1,
            grid=(db, n_pages // pages_per_step),
            in_specs=[pspec(t) for t in range(pages_per_step)],
            out_specs=pl.BlockSpec((1, nb, H_B, HEAD_DIM), lambda b, n, pt: (b, 0, 0, 0))),
        out_shape=jax.ShapeDtypeStruct((db, nb, H_B, HEAD_DIM), F32),
        compiler_params=_params("arbitrary", "arbitrary"),
        name="moba_page_sums",
    )(page_table.reshape(-1), *([ck] * pages_per_step))


def _moba_sample_select_kernel(q_ref, ks_ref, o_ref, *, own):
    q = q_ref[0]
    nb = ks_ref.shape[1]
    lane = lax.broadcasted_iota(jnp.int32, (H_B, HEAD_DIM), 1)
    sc = jnp.zeros((H_B, HEAD_DIM), F32)
    for n in range(nb):
        kmean = ks_ref[0, n] / MOBA_BLOCK
        sc = jnp.where(lane == n, jnp.sum(q * kmean, axis=-1, keepdims=True), sc)
    mask = _top3_mask(sc, jnp.full((H_B, 1), own, jnp.int32), axis=1)
    lane_f = lane.astype(F32)
    chosen = (mask == 0.0) & (lane < own)
    out = jnp.full((H_B, HEAD_DIM), -1.0, F32)
    cur = jnp.where(chosen, lane_f, float(HEAD_DIM))
    for r in range(MOBA_TOPK):
        idx = jnp.min(cur, axis=-1, keepdims=True)
        out = jnp.where(lane == r, jnp.where(idx < HEAD_DIM, idx, -1.0), out)
        cur = jnp.where(cur == idx, float(HEAD_DIM), cur)
    o_ref[0] = out.astype(jnp.int32)


def _moba_sample_select(q, ksum, own):
    db, nb = ksum.shape[:2]
    return pl.pallas_call(
        functools.partial(_moba_sample_select_kernel, own=own),
        grid=(db,),
        in_specs=[pl.BlockSpec((1, H_B, HEAD_DIM), lambda b: (b, 0, 0)),
                  pl.BlockSpec((1, nb, H_B, HEAD_DIM), lambda b: (b, 0, 0, 0))],
        out_specs=pl.BlockSpec((1, H_B, HEAD_DIM), lambda b: (b, 0, 0)),
        out_shape=jax.ShapeDtypeStruct((db, H_B, HEAD_DIM), jnp.int32),
        compiler_params=_params("arbitrary"),
        name="moba_sample_select",
    )(q, ksum)


def _moba_sample_attn_kernel(pages_ref, lpage_ref, q_ref, kn_ref, vn_ref, bias_ref, bias0_ref, k_hbm, v_hbm,
                             o_ref, kbuf, vbuf, sem, *, layer, n_slots):
    b = pl.program_id(0)
    slot = b % 2

    def copies(seq, buf):
        out = []
        for h in range(H_B):
            for t in range(n_slots):
                pg = pages_ref[(seq * H_B + h) * n_slots + t]
                out.append(pltpu.make_async_copy(k_hbm.at[layer, pg, :, h, :], kbuf.at[buf, h, t], sem.at[buf]))
                out.append(pltpu.make_async_copy(v_hbm.at[layer, pg, :, h, :], vbuf.at[buf, h, t], sem.at[buf]))
        return out

    @pl.when(b == 0)
    def _():
        for cp in copies(0, 0):
            cp.start()

    @pl.when(b + 1 < pl.num_programs(0))
    def _():
        for cp in copies(b + 1, 1 - slot):
            cp.start()

    for cp in copies(b, slot):
        cp.wait()
    lane = lax.broadcasted_iota(jnp.int32, bias_ref.shape[1:], 1)
    for h in range(H_B):
        q = q_ref[0, h:h + 1, :]
        s_list = []
        for t in range(n_slots):
            lp = lpage_ref[(b * H_B + h) * n_slots + t]
            s = jnp.sum(kbuf[slot, h, t] * q, axis=-1, keepdims=True) * ATTN_SCALE
            bias = jnp.sum(jnp.where(lane == lp, bias_ref[h], 0.0), axis=-1, keepdims=True)
            s_list.append(jnp.where(lp >= 0, s + bias, NEG_INF))
        s_new = jnp.sum(q * kn_ref[0, h:h + 1, :], axis=-1, keepdims=True) * ATTN_SCALE + bias0_ref[h]
        mx = s_new
        for s in s_list:
            mx = jnp.maximum(mx, jnp.max(s, axis=0, keepdims=True))
        e_new = jnp.exp(s_new - mx)
        e_list = [jnp.exp(s - mx) for s in s_list]
        den = e_new
        for e in e_list:
            den = den + jnp.sum(e, axis=0, keepdims=True)
        acc = (e_new / den) * vn_ref[0, h:h + 1, :]
        for t in range(n_slots):
            acc = acc + jnp.sum((e_list[t] / den) * vbuf[slot, h, t], axis=0, keepdims=True)
        o_ref[0, h:h + 1, :] = acc


def _moba_sample(qkv_s, cache_k, cache_v, layer, page_table, bias_b):
    db = qkv_s.shape[0]
    page = cache_k.shape[2]
    hw = H_B * HEAD_DIM
    n_pages = page_table.shape[1]
    past_len = n_pages * page
    assert past_len % MOBA_BLOCK == 0 and MOBA_BLOCK % page == 0
    own = past_len // MOBA_BLOCK
    per = MOBA_BLOCK // page
    n_slots = MOBA_TOPK * per
    q = qkv_s[:, :hw].reshape(db, H_B, HEAD_DIM)
    k_new = qkv_s[:, hw:2 * hw].reshape(db, H_B, HEAD_DIM)
    v_new = qkv_s[:, 2 * hw:].reshape(db, H_B, HEAD_DIM)

    ksum = _moba_block_sums(cache_k, layer, page_table)
    blocks = _moba_sample_select(q, ksum, own)[:, :, :MOBA_TOPK]
    lpage = jnp.where(blocks[..., None] >= 0, blocks[..., None] * per + jnp.arange(per), -1)
    lpage = lpage.reshape(db, H_B, n_slots)
    phys = jnp.take_along_axis(page_table[:, None, :], jnp.maximum(lpage, 0).reshape(db, 1, -1), axis=2)
    key_pos = jnp.arange(past_len).reshape(n_pages, page)
    bias_tbl = _bias_lookup(bias_b, past_len - key_pos).transpose(2, 1, 0)
    bias0 = bias_b[0].astype(F32).reshape(H_B, 1, 1)

    tok = pl.BlockSpec((1, H_B, HEAD_DIM), lambda b, pg, lp: (b, 0, 0))
    out = pl.pallas_call(
        functools.partial(_moba_sample_attn_kernel, layer=layer, n_slots=n_slots),
        grid_spec=pltpu.PrefetchScalarGridSpec(
            num_scalar_prefetch=2,
            grid=(db,),
            in_specs=[tok, tok, tok,
                      pl.BlockSpec((H_B, page, n_pages), lambda b, pg, lp: (0, 0, 0)),
                      pl.BlockSpec((H_B, 1, 1), lambda b, pg, lp: (0, 0, 0)),
                      pl.BlockSpec(memory_space=pl.ANY),
                      pl.BlockSpec(memory_space=pl.ANY)],
            out_specs=tok,
            scratch_shapes=[pltpu.VMEM((2, H_B, n_slots, page, HEAD_DIM), F32),
                            pltpu.VMEM((2, H_B, n_slots, page, HEAD_DIM), F32),
                            pltpu.SemaphoreType.DMA((2,))]),
        out_shape=jax.ShapeDtypeStruct((db, H_B, HEAD_DIM), F32),
        compiler_params=_params("arbitrary"),
        name="moba_sample_attn",
    )(phys.reshape(-1), lpage.reshape(-1), q, k_new, v_new, bias_tbl, bias0, cache_k, cache_v)
    return out.reshape(db, hw)


def kernel(x_prompt, x_sample, c_prompt, c_sample, cache_a_w128, cache_a_w512, cache_a_w2048, cache_b_k, cache_b_v, page_table, rel_bias, norm_g, w_ada, b_ada, w_ffn_in, w_ffn_out, w_qkv_a, qk_gain_a, w_o_a, w_qkv_b, qk_gain_b, w_o_b):
    batch, seq, d = x_prompt.shape
    db, dec_seq, _ = x_sample.shape
    assert batch == 1 and dec_seq == 1, "one prompt sequence, one new token per decode sequence"
    depth = norm_g.shape[0]
    n_ab = N_GROUPS_A * H_A
    bias_a = rel_bias[:, :n_ab].reshape(N_BUCKETS, N_GROUPS_A, H_A)
    bias_b = rel_bias[:, n_ab:]
    a_bufs = (cache_a_w128, cache_a_w512, cache_a_w2048)
    hw_b = H_B * HEAD_DIM

    rows = batch + db
    c_all = jnp.pad(jnp.concatenate([c_prompt, c_sample], axis=0), ((0, -rows % 8), (0, 0)))
    mod_all = _ada_all(c_all, w_ada, b_ada)

    xp = x_prompt.reshape(seq, d)
    xs = x_sample.reshape(db, d)
    a_new_p = [[] for _ in A_GROUPS]
    a_new_s = [[] for _ in A_GROUPS]
    bk_p, bk_s, bv_p, bv_s = [], [], [], []
    for layer in range(depth):
        mp = mod_all[layer, :batch]
        ms = mod_all[layer, batch:rows]
        hp, hs = _ffn_in(xp, xs, norm_g[layer, 0], mp, ms, 0, w_ffn_in, (layer, 0))
        xp, xs = _mm_res(hp, hs, w_ffn_out, (layer, 0), xp, xs, mp, ms, 0, FFN_RES)
        i = layer // 2
        if layer % 2 == 0:
            qkv_p, qkv_s = _qkv(xp, xs, norm_g[layer, 1], mp, ms, w_qkv_a, (i,), qk_gain_a)
            yp = _dilated_prompt(qkv_p, bias_a)
            ys = _dilated_sample(qkv_s, a_bufs, i, bias_a)
            w_o = w_o_a
            kv_p = qkv_p.reshape(seq, 3, N_GROUPS_A, H_A, HEAD_DIM)
            kv_s = qkv_s.reshape(db, 3, N_GROUPS_A, H_A, HEAD_DIM)
            for g, (win, dil) in enumerate(A_GROUPS):
                keep = min(win, seq)
                a_new_p[g].append(jnp.stack([kv_p[seq - keep:, 1, g], kv_p[seq - keep:, 2, g]], axis=2)[None])
                a_new_s[g].append(jnp.stack([kv_s[:, 1, g], kv_s[:, 2, g]], axis=2)[:, None])
        else:
            qkv_p, qkv_p16, qkv_s = _qkv(xp, xs, norm_g[layer, 1], mp, ms, w_qkv_b, (i,), qk_gain_b,
                                         with_bf16=True)
            yp = _moba_prompt(qkv_p, qkv_p16, bias_b)
            ys = _moba_sample(qkv_s, cache_b_k, cache_b_v, i, page_table, bias_b)
            w_o = w_o_b
            k_heads, v_heads = _kv_heads(qkv_p)
            bk_p.append(k_heads[None])
            bv_p.append(v_heads[None])
            bk_s.append(qkv_s[:, hw_b:2 * hw_b].reshape(db, dec_seq, H_B, HEAD_DIM))
            bv_s.append(qkv_s[:, 2 * hw_b:].reshape(db, dec_seq, H_B, HEAD_DIM))
        xp, xs = _mm_res(yp, ys, w_o, (i,), xp, xs, mp, ms, 1, 1.0)
        hp, hs = _ffn_in(xp, xs, norm_g[layer, 2], mp, ms, 2, w_ffn_in, (layer, 1))
        xp, xs = _mm_res(hp, hs, w_ffn_out, (layer, 1), xp, xs, mp, ms, 2, FFN_RES)
    return (xp.reshape(batch, seq, d), xs.reshape(db, dec_seq, d),
            jnp.stack(a_new_p[0]), jnp.stack(a_new_s[0]), jnp.stack(a_new_p[1]), jnp.stack(a_new_s[1]),
            jnp.stack(a_new_p[2]), jnp.stack(a_new_s[2]),
            jnp.stack(bk_p), jnp.stack(bk_s), jnp.stack(bv_p), jnp.stack(bv_s))
```
